```python
import math
import jax, jax.numpy as jnp
from jax import lax
import numpy as np

D_MODEL = 1024
BATCH = 8
SEQ = 2048
DEPTH = 2

HEAD_DIM = 64
SSD_WIDTH = D_MODEL // 2
SSD_HEADS = SSD_WIDTH // HEAD_DIM
SSD_GROUPS = 2
SSD_STATE = 128
SSD_CONV = 4
SSD_CHUNK = 128
SSD_CONV_DIM = SSD_WIDTH + 2 * SSD_GROUPS * SSD_STATE
SB_WIDTH = D_MODEL // 4
SB_HEADS = SB_WIDTH // HEAD_DIM
SB_BLOCK = 128
NSA_WIDTH = D_MODEL // 4
NSA_HEADS = NSA_WIDTH // HEAD_DIM
CMP_LEN = 32
CMP_STRIDE = 16
SEL_BLOCK = 64
SEL_TOPK = 16
WINDOW = 512
NSA_QBLOCK = 128
FORCE_SCORE = 1e9
MIX_WIDTH = SSD_WIDTH + SB_WIDTH + NSA_WIDTH
ROT_DIM = HEAD_DIM // 4
ROPE_THETA = 500000.0
PEER_HEADS = 8
N_KEYS = 128
N_EXPERTS = N_KEYS * N_KEYS
PEER_TOPK = 16
PEER_QDIM = 256
PEER_TOKEN_BLOCK = 128
NORM_EPS = 1e-6
IN_SIZES = (SSD_WIDTH, SSD_CONV_DIM, SSD_HEADS, 3 * SB_WIDTH, NSA_WIDTH, 6 * HEAD_DIM, 3 * NSA_HEADS)
IN_DIM = sum(IN_SIZES)
IN_SPLITS = tuple(sum(IN_SIZES[:i + 1]) for i in range(len(IN_SIZES) - 1))

kernel_name = "hybrid_ssd_stickbreak_nsa_peer"


def rms_norm(x, w):
    xf = x.astype(jnp.float32)
    y = xf * lax.rsqrt(jnp.mean(xf * xf, axis=-1, keepdims=True) + NORM_EPS)
    return (y * w.astype(jnp.float32)).astype(x.dtype)


def rope_tables(seq_len):
    pos = jnp.arange(seq_len, dtype=jnp.float32)
    inv_freq = ROPE_THETA ** (-jnp.arange(0, ROT_DIM, 2, dtype=jnp.float32) / ROT_DIM)
    ang = pos[:, None] * inv_freq[None, :]
    return jnp.cos(ang), jnp.sin(ang)


def partial_rope(x, cos, sin):
    half = ROT_DIM // 2
    x1, x2, rest = x[..., :half], x[..., half:ROT_DIM], x[..., ROT_DIM:]
    c, s = cos.astype(x.dtype), sin.astype(x.dtype)
    return jnp.concatenate([x1 * c - x2 * s, x2 * c + x1 * s, rest], axis=-1)


def causal_depthwise_conv(x, w):
    return lax.conv_general_dilated(
        x, w[:, None, :], window_strides=(1,), padding=[(w.shape[0] - 1, 0)],
        dimension_numbers=('NWC', 'WIO', 'NWC'), feature_group_count=x.shape[-1])


def segsum(x):
    T = x.shape[-1]
    xr = jnp.broadcast_to(x[..., :, None], x.shape + (T,))
    strict = jnp.tril(jnp.ones((T, T), dtype=bool), -1)
    ss = jnp.cumsum(jnp.where(strict, xr, 0.0), axis=-2)
    return jnp.where(jnp.tril(jnp.ones((T, T), dtype=bool)), ss, -jnp.inf)


def ssd_mixer(z, xbc, dt_raw, conv_w, conv_b, dt_bias, a_log, d_skip, norm_w):
    Bsz, S, _ = z.shape
    G, R, P, N, L = SSD_GROUPS, SSD_HEADS // SSD_GROUPS, HEAD_DIM, SSD_STATE, SSD_CHUNK
    nc = S // L
    xbc = jax.nn.silu(causal_depthwise_conv(xbc, conv_w) + conv_b)
    xs, b_in, c_in = jnp.split(xbc, [SSD_WIDTH, SSD_WIDTH + G * N], axis=-1)
    dt = jax.nn.softplus((dt_raw + dt_bias).astype(jnp.float32))
    a = -jnp.exp(a_log.astype(jnp.float32))
    xh = xs.reshape(Bsz, S, SSD_HEADS, P)
    X = (xh * dt[..., None]).reshape(Bsz, nc, L, G, R, P)
    Bm = b_in.reshape(Bsz, nc, L, G, N)
    Cm = c_in.reshape(Bsz, nc, L, G, N)
    A = (dt * a).reshape(Bsz, nc, L, G, R).transpose(0, 3, 4, 1, 2)
    A_cs = jnp.cumsum(A, axis=-1)
    Lmat = jnp.exp(segsum(A))
    CB = jnp.einsum('bclgn,bcsgn->bgcls', Cm, Bm)
    y_diag = jnp.einsum('bgrcls,bcsgrp->bclgrp', Lmat * CB[:, :, None], X)
    decay_states = jnp.exp(A_cs[..., -1:] - A_cs).transpose(0, 3, 4, 1, 2)
    states = jnp.einsum('bclgn,bclgrp->bcgrpn', Bm, X * decay_states[..., None])
    states = jnp.concatenate([jnp.zeros_like(states[:, :1]), states], axis=1)
    A_last = jnp.pad(A_cs[..., -1], ((0, 0), (0, 0), (0, 0), (1, 0)))
    decay_chunk = jnp.exp(segsum(A_last))
    states = jnp.einsum('bgrzc,bcgrpn->bzgrpn', decay_chunk, states)[:, :-1]
    y_off = jnp.einsum('bclgn,bcgrpn->bclgrp', Cm, states) * jnp.exp(A_cs).transpose(0, 3, 4, 1, 2)[..., None]
    y = (y_diag + y_off).reshape(Bsz, S, SSD_HEADS, P) + xh * d_skip[:, None]
    y = y.reshape(Bsz, S, SSD_WIDTH)
    return rms_norm(y * jax.nn.silu(z), norm_w)


def stick_breaking_attention(qkv):
    Bsz, S, _ = qkv.shape
    H, d = SB_HEADS, HEAD_DIM
    q, k, v = [u.reshape(Bsz, S, H, d).transpose(0, 2, 1, 3) for u in jnp.split(qkv, 3, axis=-1)]
    nqb = S // SB_BLOCK
    q_blocks = q.reshape(Bsz, H, nqb, SB_BLOCK, d).transpose(2, 0, 1, 3, 4)
    starts = jnp.arange(nqb) * SB_BLOCK
    kpos = jnp.arange(S)
    scale = d ** -0.5

    def block(args):
        qi, t0 = args
        tq = t0 + jnp.arange(SB_BLOCK)
        mask = kpos[None, :] < tq[:, None]
        z = (jnp.einsum('bhqd,bhkd->bhqk', qi, k) * scale).astype(jnp.float32)
        log_beta = jnp.where(mask, jax.nn.log_sigmoid(z), -jnp.inf)
        log_stay = jnp.where(mask, jax.nn.log_sigmoid(-z), 0.0)
        log_after = lax.cumsum(log_stay, axis=log_stay.ndim - 1, reverse=True) - log_stay
        w = jnp.exp(log_beta + log_after)
        return jnp.einsum('bhqk,bhkd->bhqd', w.astype(v.dtype), v)

    o = lax.map(block, (q_blocks, starts))
    return o.transpose(1, 0, 3, 2, 4).reshape(Bsz, S, H * d)


def nsa_attention(q, kv, gate_logits, q_norm_w, k_norm_w, cmp_pos, cmp_w, cos, sin):
    Bsz, S, _ = q.shape
    H, d = NSA_HEADS, HEAD_DIM
    scale = d ** -0.5
    t = jnp.arange(S)
    kc, vc, ks, vs, kw, vw = jnp.split(kv, 6, axis=-1)
    q = partial_rope(rms_norm(q.reshape(Bsz, S, H, d), q_norm_w).transpose(0, 2, 1, 3), cos, sin)

    n_cmp = (S - CMP_LEN) // CMP_STRIDE + 1
    cmp_start = np.arange(n_cmp) * CMP_STRIDE
    tok = cmp_start[:, None] + np.arange(CMP_LEN)[None, :]
    kc_blk = (partial_rope(kc, cos, sin)[:, tok] + cmp_pos[0]).reshape(Bsz, n_cmp, CMP_LEN * d)
    k_cmp = rms_norm(kc_blk @ cmp_w[0], k_norm_w[0])
    v_cmp = (vc[:, tok] + cmp_pos[1]).reshape(Bsz, n_cmp, CMP_LEN * d) @ cmp_w[1]
    cmp_visible = jnp.asarray(cmp_start + CMP_LEN - 1)[None, :] <= t[:, None]
    s_cmp = (jnp.einsum('bhsd,bcd->bhsc', q, k_cmp) * scale).astype(jnp.float32)
    p_cmp = jax.nn.softmax(jnp.where(cmp_visible, s_cmp, -1e30), axis=-1) * cmp_visible
    o_cmp = jnp.einsum('bhsc,bcd->bhsd', p_cmp.astype(v_cmp.dtype), v_cmp)

    n_sel = S // SEL_BLOCK
    sel_start = np.arange(n_sel) * SEL_BLOCK
    overlap = ((cmp_start[:, None] < sel_start[None, :] + SEL_BLOCK)
               & (cmp_start[:, None] + CMP_LEN > sel_start[None, :])).astype(np.float32)
    imp = jnp.einsum('bhsc,cj->bsj', p_cmp, jnp.asarray(overlap))
    j = jnp.arange(n_sel)[None, :]
    blk_t = (t // SEL_BLOCK)[:, None]
    forced = (j == 0) | (j == blk_t) | (j == blk_t - 1)
    causal = j * SEL_BLOCK <= t[:, None]
    imp = jnp.where(causal, jnp.where(forced, FORCE_SCORE, imp), -jnp.inf)
    n_top = min(SEL_TOPK, n_sel)
    _, sel_idx = lax.top_k(imp, n_top)

    ks = partial_rope(rms_norm(ks, k_norm_w[1]), cos, sin)
    kw = partial_rope(rms_norm(kw, k_norm_w[2]), cos, sin)
    ks_blk = ks.reshape(Bsz, n_sel, SEL_BLOCK, d)
    vs_blk = vs.reshape(Bsz, n_sel, SEL_BLOCK, d)
    kw_pad = jnp.pad(kw, ((0, 0), (WINDOW, 0), (0, 0)))
    vw_pad = jnp.pad(vw, ((0, 0), (WINDOW, 0), (0, 0)))
    nqb = S // NSA_QBLOCK
    q_blocks = q.reshape(Bsz, H, nqb, NSA_QBLOCK, d).transpose(2, 0, 1, 3, 4)
    idx_blocks = sel_idx.reshape(Bsz, nqb, NSA_QBLOCK, n_top).transpose(1, 0, 2, 3)
    starts = jnp.arange(nqb) * NSA_QBLOCK
    take_blocks = jax.vmap(lambda blocks, idx: blocks[idx])
    n_win = WINDOW + NSA_QBLOCK

    def block(args):
        qi, idx, t0 = args
        tq = t0 + jnp.arange(NSA_QBLOCK)
        k_g = take_blocks(ks_blk, idx).reshape(Bsz, NSA_QBLOCK, n_top * SEL_BLOCK, d)
        v_g = take_blocks(vs_blk, idx).reshape(Bsz, NSA_QBLOCK, n_top * SEL_BLOCK, d)
        kpos = (idx[..., None] * SEL_BLOCK + jnp.arange(SEL_BLOCK)).reshape(Bsz, NSA_QBLOCK, n_top * SEL_BLOCK)
        m_sel = (kpos <= tq[None, :, None])[:, None]
        s_sel = (jnp.einsum('bhqd,bqkd->bhqk', qi, k_g) * scale).astype(jnp.float32)
        p_sel = jax.nn.softmax(jnp.where(m_sel, s_sel, -1e30), axis=-1)
        o_sel = jnp.einsum('bhqk,bqkd->bhqd', p_sel.astype(v_g.dtype), v_g)
        k_w = lax.dynamic_slice_in_dim(kw_pad, t0, n_win, axis=1)
        v_w = lax.dynamic_slice_in_dim(vw_pad, t0, n_win, axis=1)
        kpos_w = t0 - WINDOW + jnp.arange(n_win)
        rel = tq[:, None] - kpos_w[None, :]
        m_win = (rel >= 0) & (rel < WINDOW) & (kpos_w[None, :] >= 0)
        s_win = (jnp.einsum('bhqd,bkd->bhqk', qi, k_w) * scale).astype(jnp.float32)
        p_win = jax.nn.softmax(jnp.where(m_win, s_win, -1e30), axis=-1)
        o_win = jnp.einsum('bhqk,bkd->bhqd', p_win.astype(v_w.dtype), v_w)
        return o_sel, o_win

    o_sel, o_win = lax.map(block, (q_blocks, idx_blocks, starts))
    unblock = lambda o: o.transpose(1, 2, 0, 3, 4).reshape(Bsz, H, S, d)
    g = jax.nn.sigmoid(gate_logits.reshape(Bsz, S, H, 3).astype(jnp.float32)).transpose(0, 2, 1, 3)
    o = g[..., 0:1] * o_cmp + g[..., 1:2] * unblock(o_sel) + g[..., 2:3] * unblock(o_win)
    return o.transpose(0, 2, 1, 3).reshape(Bsz, S, NSA_WIDTH)


def peer_ffn(h, w_q, sub_keys, u_tab, v_tab):
    Bsz, S, D = h.shape
    T = Bsz * S
    Tb = PEER_TOKEN_BLOCK
    half = PEER_QDIM // 2

    def block(xb):
        q = (xb @ w_q).reshape(Tb, PEER_HEADS, 2, half)
        s = jnp.einsum('thpd,pkd->thpk', q, sub_keys).astype(jnp.float32)
        sv, si = lax.top_k(s, PEER_TOPK)
        cand = sv[:, :, 0, :, None] + sv[:, :, 1, None, :]
        cv, ci = lax.top_k(cand.reshape(Tb, PEER_HEADS, PEER_TOPK * PEER_TOPK), PEER_TOPK)
        i1 = jnp.take_along_axis(si[:, :, 0, :], ci // PEER_TOPK, axis=-1)
        i2 = jnp.take_along_axis(si[:, :, 1, :], ci % PEER_TOPK, axis=-1)
        eidx = i1 * N_KEYS + i2
        g = jax.nn.softmax(cv, axis=-1)
        u = u_tab[eidx]
        v = v_tab[eidx]
        a = jax.nn.gelu(jnp.einsum('thkd,td->thk', u, xb).astype(jnp.float32), approximate=False)
        return jnp.einsum('thk,thkd->td', (g * a).astype(v.dtype), v)

    out = lax.map(block, h.reshape(T // Tb, Tb, D))
    return out.reshape(Bsz, S, D)


def setup_inputs(seed: int = 0) -> dict:
    key = jax.random.key(seed)
    ks = jax.random.split(key, 24)
    L, D = DEPTH, D_MODEL
    nrm = lambda k, shape, scale: jax.random.normal(k, shape, jnp.float32) * scale
    dt0 = jnp.exp(jax.random.uniform(ks[5], (L, SSD_HEADS), jnp.float32, math.log(1e-3), math.log(1e-1)))
    return {
        "x": nrm(ks[0], (BATCH, SEQ, D), 1.0),
        "norm1_w": 1.0 + nrm(ks[1], (L, D), 0.02),
        "w_in": nrm(ks[2], (L, D, IN_DIM), D ** -0.5),
        "conv_w": nrm(ks[3], (L, SSD_CONV, SSD_CONV_DIM), SSD_CONV ** -0.5),
        "conv_b": nrm(ks[4], (L, SSD_CONV_DIM), 0.02),
        "dt_bias": dt0 + jnp.log(-jnp.expm1(-dt0)),
        "a_log": jnp.log(jax.random.uniform(ks[6], (L, SSD_HEADS), jnp.float32, 1.0, 16.0)),
        "d_skip": 1.0 + nrm(ks[7], (L, SSD_HEADS), 0.02),
        "ssd_norm_w": 1.0 + nrm(ks[8], (L, SSD_WIDTH), 0.02),
        "sb_out_w": 1.0 + nrm(ks[9], (L, SB_WIDTH), 0.02),
        "nsa_q_norm_w": 1.0 + nrm(ks[10], (L, HEAD_DIM), 0.02),
        "nsa_k_norm_w": 1.0 + nrm(ks[11], (L, 3, HEAD_DIM), 0.02),
        "nsa_cmp_pos": nrm(ks[12], (L, 2, CMP_LEN, HEAD_DIM), 0.1),
        "nsa_cmp_w": nrm(ks[13], (L, 2, CMP_LEN * HEAD_DIM, HEAD_DIM), (CMP_LEN * HEAD_DIM) ** -0.5),
        "nsa_out_w": 1.0 + nrm(ks[14], (L, NSA_WIDTH), 0.02),
        "w_out": nrm(ks[15], (L, MIX_WIDTH, D), MIX_WIDTH ** -0.5),
        "norm2_w": 1.0 + nrm(ks[16], (L, D), 0.02),
        "peer_w_q": nrm(ks[17], (L, D, PEER_HEADS * PEER_QDIM), D ** -0.5),
        "peer_sub_keys": nrm(ks[18], (L, 2, N_KEYS, PEER_QDIM // 2), (PEER_QDIM // 2) ** -0.5),
        "peer_u": nrm(ks[19], (L, N_EXPERTS, D), D ** -0.5),
        "peer_v": nrm(ks[20], (L, N_EXPERTS, D), (PEER_HEADS * PEER_TOPK) ** -0.5),
    }


def reference(x, norm1_w, w_in, conv_w, conv_b, dt_bias, a_log, d_skip, ssd_norm_w, sb_out_w,
              nsa_q_norm_w, nsa_k_norm_w, nsa_cmp_pos, nsa_cmp_w, nsa_out_w, w_out, norm2_w,
              peer_w_q, peer_sub_keys, peer_u, peer_v):
    cos, sin = rope_tables(x.shape[1])
    for l in range(DEPTH):
        h = rms_norm(x, norm1_w[l])
        proj = h @ w_in[l]
        z, xbc, dt_raw, sb_qkv, nsa_q, nsa_kv, nsa_gate = jnp.split(proj, IN_SPLITS, axis=-1)
        y_ssd = ssd_mixer(z, xbc, dt_raw, conv_w[l], conv_b[l], dt_bias[l], a_log[l], d_skip[l], ssd_norm_w[l])
        y_sb = rms_norm(stick_breaking_attention(sb_qkv), sb_out_w[l])
        y_nsa = rms_norm(nsa_attention(nsa_q, nsa_kv, nsa_gate, nsa_q_norm_w[l], nsa_k_norm_w[l],
                                       nsa_cmp_pos[l], nsa_cmp_w[l], cos, sin), nsa_out_w[l])
        mixed = jnp.concatenate([y_ssd, y_sb, y_nsa], axis=-1).astype(x.dtype) @ w_out[l]
        x = x + mixed
        x = x + peer_ffn(rms_norm(x, norm2_w[l]), peer_w_q[l], peer_sub_keys[l], peer_u[l], peer_v[l]).astype(x.dtype)
    return x
```

```python
import functools
import math

import numpy as np
import jax
import jax.numpy as jnp
from jax import lax
from jax.experimental import pallas as pl
from jax.experimental.pallas import tpu as pltpu

F32 = jnp.float32
BF16 = jnp.bfloat16

D_MODEL = 1024
HEAD_DIM = 64
SSD_WIDTH = 512
SSD_HEADS = 8
SSD_GROUPS = 2
SSD_STATE = 128
SSD_CONV = 4
SSD_CHUNK = 128
SSD_CONV_DIM = SSD_WIDTH + 2 * SSD_GROUPS * SSD_STATE
SB_WIDTH = 256
SB_HEADS = 4
NSA_WIDTH = 256
NSA_HEADS = 4
CMP_LEN = 32
CMP_STRIDE = 16
SEL_BLOCK = 64
SEL_TOPK = 16
WINDOW = 512
FORCE_SCORE = 1e9
ROT_DIM = 16
ROPE_THETA = 500000.0
PEER_HEADS = 8
N_KEYS = 128
PEER_TOPK = 16
PEER_QDIM = 256
NORM_EPS = 1e-6
NEG_BIG = -1e30

LANE = 128
SUBLANE = 8
V7X_VMEM_BYTES = 64 * 1024 * 1024
VMEM_LIMIT = V7X_VMEM_BYTES - 12 * 1024 * 1024

SLOT = LANE
Q_TILE = 128


def _cparams(*sem):
    return pltpu.CompilerParams(dimension_semantics=sem, vmem_limit_bytes=VMEM_LIMIT)


def _iota(shape, dim):
    return lax.broadcasted_iota(jnp.int32, shape, dim)


def _dot(a, b):
    return jnp.dot(a.astype(BF16), b.astype(BF16), preferred_element_type=F32)


def _dot_nt(a, b):
    return lax.dot_general(a.astype(BF16), b.astype(BF16), (((1,), (1,)), ((), ())),
                           preferred_element_type=F32)


def _split(a, n):
    parts, r = [], a
    for i in range(n):
        p = r.astype(BF16)
        parts.append(p)
        if i + 1 < n:
            r = r - p.astype(F32)
    return parts


def _dot_split_lhs(a, b_exact, n=3):
    acc = None
    for p in _split(a, n):
        t = jnp.dot(p, b_exact, preferred_element_type=F32)
        acc = t if acc is None else acc + t
    return acc


def _dot_split_rhs(a_exact, b, n=3):
    acc = None
    for p in _split(b, n):
        t = jnp.dot(a_exact, p, preferred_element_type=F32)
        acc = t if acc is None else acc + t
    return acc


def _dot_hi(a, b):
    a1, a2 = _split(a, 2)
    b1, b2 = _split(b, 2)
    return (jnp.dot(a1, b1, preferred_element_type=F32) + jnp.dot(a1, b2, preferred_element_type=F32)
            + jnp.dot(a2, b1, preferred_element_type=F32))


def _sigmoid(x):
    return 1.0 / (1.0 + jnp.exp(-x))


def _softplus(x):
    return jnp.maximum(x, 0.0) + jnp.log1p(jnp.exp(-jnp.abs(x)))


def _rms(x, w, n):
    ms = jnp.sum(x * x, axis=-1, keepdims=True) * (1.0 / n)
    return x * lax.rsqrt(ms + NORM_EPS) * w


def _softmax_rows(s):
    m = jnp.max(s, axis=-1, keepdims=True)
    e = jnp.exp(s - m)
    return e / jnp.sum(e, axis=-1, keepdims=True)


_IN_SEGS = (("z", 512, F32), ("xbc", 1024, F32), ("dt", 128, F32),
            ("sbq", 512, BF16), ("sbk", 512, BF16), ("sbv", 512, BF16),
            ("nq", 512, F32), ("nkv", 768, F32), ("gate", 128, F32))
_IN_TOTAL = sum(s[1] for s in _IN_SEGS)


def _inproj_colmap():
    src = []

    def real(a, n, pad_to=None):
        src.extend(range(a, a + n))
        if pad_to:
            src.extend([-1] * (pad_to - n))

    def heads(a, nheads):
        for h in range(nheads):
            src.extend(range(a + h * HEAD_DIM, a + (h + 1) * HEAD_DIM))
            src.extend([-1] * (SLOT - HEAD_DIM))

    o_xbc = SSD_WIDTH
    o_dt = o_xbc + SSD_CONV_DIM
    o_sb = o_dt + SSD_HEADS
    o_nq = o_sb + 3 * SB_WIDTH
    o_nkv = o_nq + NSA_WIDTH
    o_gate = o_nkv + 6 * HEAD_DIM
    real(0, SSD_WIDTH)
    real(o_xbc, SSD_CONV_DIM)
    real(o_dt, SSD_HEADS, LANE)
    heads(o_sb, SB_HEADS)
    heads(o_sb + SB_WIDTH, SB_HEADS)
    heads(o_sb + 2 * SB_WIDTH, SB_HEADS)
    heads(o_nq, NSA_HEADS)
    heads(o_nkv, 6)
    real(o_gate, 3 * NSA_HEADS, LANE)
    src = np.asarray(src, np.int32)
    assert src.shape[0] == _IN_TOTAL
    return src


_IN_COLMAP = _inproj_colmap()


def _pad_cols(w, colmap):
    g = jnp.take(w, jnp.asarray(np.maximum(colmap, 0)), axis=-1)
    return jnp.where(jnp.asarray(colmap >= 0), g, 0.0)


def _pad_heads_vec(v, nheads):
    v = v.reshape(nheads, HEAD_DIM)
    v = jnp.pad(v, ((0, 0), (0, SLOT - HEAD_DIM)))
    return v.reshape(1, nheads * SLOT)


def _inproj_kernel(x_ref, nw_ref, w_ref, *out_refs):
    x = x_ref[...]
    h = _rms(x, nw_ref[...], D_MODEL).astype(BF16)
    off = 0
    for (name, width, dt), o_ref in zip(_IN_SEGS, out_refs):
        o_ref[...] = jnp.dot(h, w_ref[:, off:off + width], preferred_element_type=F32).astype(dt)
        off += width


def _inproj(xt, norm_w, w_pad):
    T = xt.shape[0]
    tm = 256
    out_shape = [jax.ShapeDtypeStruct((T, w), dt) for (_, w, dt) in _IN_SEGS]
    out_specs = [pl.BlockSpec((tm, w), lambda i: (i, 0)) for (_, w, _) in _IN_SEGS]
    return pl.pallas_call(
        _inproj_kernel,
        grid=(T // tm,),
        in_specs=[pl.BlockSpec((tm, D_MODEL), lambda i: (i, 0)),
                  pl.BlockSpec((1, D_MODEL), lambda i: (0, 0)),
                  pl.BlockSpec((D_MODEL, _IN_TOTAL), lambda i: (0, 0))],
        out_specs=out_specs,
        out_shape=out_shape,
        compiler_params=_cparams("parallel"),
    )(xt, norm_w.reshape(1, D_MODEL), w_pad)


def _ssd_kernel(z_ref, xbc_ref, dt_ref, cw_ref, cb_ref, dtb_ref, alog_ref, dsk_ref, nw_ref,
                o_ref, xbuf, state):
    L = SSD_CHUNK
    c = pl.program_id(1)

    @pl.when(c == 0)
    def _():
        xbuf[0:SUBLANE, :] = jnp.zeros((SUBLANE, SSD_CONV_DIM), F32)
        state[...] = jnp.zeros_like(state)

    xcur = xbc_ref[...]
    xbuf[SUBLANE:SUBLANE + L, :] = xcur
    acc = jnp.broadcast_to(cb_ref[...], (L, SSD_CONV_DIM))
    for k in range(SSD_CONV):
        s0 = SUBLANE - (SSD_CONV - 1) + k
        acc = acc + xbuf[s0:s0 + L, :] * cw_ref[k:k + 1, :]
    xbuf[0:SUBLANE, :] = xcur[L - SUBLANE:L, :]
    xc = acc * _sigmoid(acc)

    xs = xc[:, 0:SSD_WIDTH]
    dt = _softplus(dt_ref[...] + dtb_ref[...])
    a_dec = dt * (-jnp.exp(alog_ref[...]))

    row = _iota((L, L), 0)
    col = _iota((L, L), 1)
    tril = row >= col
    tri = tril.astype(BF16)
    cs = _dot_split_rhs(tri, a_dec)
    e64 = (_iota((LANE, SSD_WIDTH), 0) == _iota((LANE, SSD_WIDTH), 1) // HEAD_DIM).astype(BF16)
    e128 = (_iota((LANE, SSD_HEADS * LANE), 0) == _iota((LANE, SSD_HEADS * LANE), 1) // LANE).astype(BF16)
    dt_e = _dot_split_lhs(dt, e64)
    cs_e = _dot_split_lhs(cs, e64)
    cs_e2 = _dot_split_lhs(cs, e128)
    cs_t = cs.T
    cs_last = cs_e[L - 1:L, :]
    x_dt = xs * dt_e
    x_dec = x_dt * jnp.exp(cs_last - cs_e)
    in_dec = jnp.exp(cs_e)
    chunk_dec = jnp.exp(cs_last)
    lane_lo = _iota((L, LANE), 1) < HEAD_DIM

    ys = []
    for g in range(SSD_GROUPS):
        b_g = xc[:, SSD_WIDTH + g * SSD_STATE:SSD_WIDTH + (g + 1) * SSD_STATE]
        c_g = xc[:, SSD_WIDTH + (SSD_GROUPS + g) * SSD_STATE:SSD_WIDTH + (SSD_GROUPS + g + 1) * SSD_STATE]
        cb = _dot_nt(c_g, b_g)
        b_gt = b_g.T
        for m in range(2 * g, 2 * g + 2):
            lanes = slice(m * LANE, (m + 1) * LANE)
            xp = x_dt[:, lanes]
            yd = []
            for h in (2 * m, 2 * m + 1):
                diff = cs_e2[:, h * LANE:(h + 1) * LANE] - cs_t[h:h + 1, :]
                lm = jnp.exp(jnp.where(tril, diff, -jnp.inf))
                yd.append(_dot(lm * cb, xp))
            y_diag = jnp.where(lane_lo, yd[0], yd[1])
            s_prev = state[m]
            y_off = _dot(c_g, s_prev) * in_dec[:, lanes]
            state[m] = s_prev * chunk_dec[:, lanes] + _dot(b_gt, x_dec[:, lanes])
            ys.append(y_diag + y_off + xs[:, lanes] * dsk_ref[:, lanes])
    y = jnp.concatenate(ys, axis=1)
    zt = z_ref[...]
    o_ref[...] = _rms(y * (zt * _sigmoid(zt)), nw_ref[...], SSD_WIDTH)


def _ssd(z, xbc, dt, conv_w, conv_b, dt_bias, a_log, d_skip, norm_w, B, S):
    T = B * S
    L = SSD_CHUNK
    nc = S // L
    pad8 = lambda v: jnp.pad(v, (0, LANE - SSD_HEADS)).reshape(1, LANE)
    row = lambda b, c: (b * nc + c, 0)
    fixed = lambda b, c: (0, 0)
    return pl.pallas_call(
        _ssd_kernel,
        grid=(B, nc),
        in_specs=[pl.BlockSpec((L, SSD_WIDTH), row),
                  pl.BlockSpec((L, SSD_CONV_DIM), row),
                  pl.BlockSpec((L, LANE), row),
                  pl.BlockSpec((SSD_CONV, SSD_CONV_DIM), fixed),
                  pl.BlockSpec((1, SSD_CONV_DIM), fixed),
                  pl.BlockSpec((1, LANE), fixed),
                  pl.BlockSpec((1, LANE), fixed),
                  pl.BlockSpec((1, SSD_WIDTH), fixed),
                  pl.BlockSpec((1, SSD_WIDTH), fixed)],
        out_specs=pl.BlockSpec((L, SSD_WIDTH), row),
        out_shape=jax.ShapeDtypeStruct((T, SSD_WIDTH), F32),
        scratch_shapes=[pltpu.VMEM((SUBLANE + L, SSD_CONV_DIM), F32),
                        pltpu.VMEM((SSD_HEADS // 2, SSD_STATE, LANE), F32)],
        compiler_params=_cparams("parallel", "arbitrary"),
    )(z, xbc, dt, conv_w, conv_b.reshape(1, -1), pad8(dt_bias), pad8(a_log),
      jnp.repeat(d_skip, HEAD_DIM).reshape(1, SSD_WIDTH), norm_w.reshape(1, SSD_WIDTH))


def _sb_kernel(q_ref, k_ref, v_ref, o_ref):
    tq = Q_TILE
    qi = pl.program_id(2)
    q = q_ref[...]
    row = _iota((tq, tq), 0)
    col = _iota((tq, tq), 1)
    strict = col < row
    r2 = _iota((tq, 2 * tq), 0)
    c2 = _iota((tq, 2 * tq), 1)
    umat = ((r2 > c2) | (c2 >= tq)).astype(BF16)
    scale = HEAD_DIM ** -0.5

    def body(i, carry):
        acc, cf = carry
        kj = qi - i
        ks = pl.multiple_of(kj * tq, tq)
        kt = k_ref[pl.ds(ks, tq), :]
        vt = v_ref[pl.ds(ks, tq), :]
        z = _dot_nt(q, kt) * scale
        l1p = jnp.log1p(jnp.exp(-jnp.abs(z)))
        log_beta = jnp.minimum(z, 0.0) - l1p
        log_stay = -jnp.maximum(z, 0.0) - l1p
        mask = jnp.logical_or(i > 0, strict)
        log_stay = jnp.where(mask, log_stay, 0.0)
        log_beta = jnp.where(mask, log_beta, -jnp.inf)
        r = _dot_split_lhs(log_stay, umat, 2)
        w = jnp.exp(log_beta + r[:, 0:tq] + cf)
        acc = acc + _dot(w, vt)
        cf = cf + r[:, tq:2 * tq]
        return acc, cf

    zero = jnp.zeros((tq, SLOT), F32)
    acc, _ = lax.fori_loop(0, qi + 1, body, (zero, zero))
    o_ref[...] = acc


def _sb(sbq, sbk, sbv, B, S):
    T = B * S
    nq = S // Q_TILE
    H = SB_HEADS
    return pl.pallas_call(
        _sb_kernel,
        grid=(B, H, nq),
        in_specs=[pl.BlockSpec((Q_TILE, SLOT), lambda b, h, i: (b * nq + i, h)),
                  pl.BlockSpec((S, SLOT), lambda b, h, i: (b, h)),
                  pl.BlockSpec((S, SLOT), lambda b, h, i: (b, h))],
        out_specs=pl.BlockSpec((Q_TILE, SLOT), lambda b, h, i: (b * nq + i, h)),
        out_shape=jax.ShapeDtypeStruct((T, H * SLOT), F32),
        compiler_params=_cparams("parallel", "parallel", "arbitrary"),
    )(sbq, sbk, sbv)


def _rope_tables(S):
    pos = jnp.arange(S, dtype=F32)
    inv_freq = ROPE_THETA ** (-jnp.arange(0, ROT_DIM, 2, dtype=F32) / ROT_DIM)
    ang = pos[:, None] * inv_freq[None, :]
    cos, sin = jnp.cos(ang), jnp.sin(ang)
    half = ROT_DIM // 2
    ones = jnp.ones((S, SLOT - ROT_DIM), F32)
    zeros_h = jnp.zeros((S, half), F32)
    zeros_r = jnp.zeros((S, SLOT - ROT_DIM), F32)
    cosf = jnp.concatenate([cos, cos, ones], axis=1)
    sin_a = jnp.concatenate([-sin, zeros_h, zeros_r], axis=1)
    sin_b = jnp.concatenate([zeros_h, sin, zeros_r], axis=1)
    return cosf, sin_a, sin_b


def _nsa_prep_kernel(q_ref, kv_ref, cos_ref, sa_ref, sb_ref, qw_ref, kw_ref,
                     qo_ref, kc_ref, vc_ref, ksw_ref):
    cos, sa, sb = cos_ref[...], sa_ref[...], sb_ref[...]
    half = ROT_DIM // 2

    def rope(x):
        return x * cos + pltpu.roll(x, SLOT - half, 1) * sa + pltpu.roll(x, half, 1) * sb

    qw = qw_ref[...]
    for h in range(NSA_HEADS):
        lanes = slice(h * SLOT, (h + 1) * SLOT)
        qo_ref[:, lanes] = rope(_rms(q_ref[:, lanes], qw, HEAD_DIM)).astype(BF16)
    kc_ref[...] = rope(kv_ref[:, 0:SLOT])
    vc_ref[...] = kv_ref[:, SLOT:2 * SLOT]
    ksw_ref[:, 0:SLOT] = rope(_rms(kv_ref[:, 2 * SLOT:3 * SLOT], kw_ref[1:2, :], HEAD_DIM)).astype(BF16)
    ksw_ref[:, SLOT:2 * SLOT] = kv_ref[:, 3 * SLOT:4 * SLOT].astype(BF16)
    ksw_ref[:, 2 * SLOT:3 * SLOT] = rope(_rms(kv_ref[:, 4 * SLOT:5 * SLOT], kw_ref[2:3, :], HEAD_DIM)).astype(BF16)
    ksw_ref[:, 3 * SLOT:4 * SLOT] = kv_ref[:, 5 * SLOT:6 * SLOT].astype(BF16)


def _nsa_prep(nq, nkv, ropes, q_norm_w, k_norm_w, B, S):
    T = B * S
    ts = 256
    nt = S // ts
    cosf, sin_a, sin_b = ropes
    pad = lambda v: jnp.pad(v, ((0, 0), (0, SLOT - HEAD_DIM)))
    row = lambda b, i: (b * nt + i, 0)
    pos = lambda b, i: (i, 0)
    fixed = lambda b, i: (0, 0)
    return pl.pallas_call(
        _nsa_prep_kernel,
        grid=(B, nt),
        in_specs=[pl.BlockSpec((ts, NSA_HEADS * SLOT), row),
                  pl.BlockSpec((ts, 6 * SLOT), row),
                  pl.BlockSpec((ts, SLOT), pos),
                  pl.BlockSpec((ts, SLOT), pos),
                  pl.BlockSpec((ts, SLOT), pos),
                  pl.BlockSpec((1, SLOT), fixed),
                  pl.BlockSpec((3, SLOT), fixed)],
        out_specs=[pl.BlockSpec((ts, NSA_HEADS * SLOT), row),
                   pl.BlockSpec((ts, SLOT), row),
                   pl.BlockSpec((ts, SLOT), row),
                   pl.BlockSpec((ts, 4 * SLOT), row)],
        out_shape=[jax.ShapeDtypeStruct((T, NSA_HEADS * SLOT), BF16),
                   jax.ShapeDtypeStruct((T, SLOT), F32),
                   jax.ShapeDtypeStruct((T, SLOT), F32),
                   jax.ShapeDtypeStruct((T, 4 * SLOT), BF16)],
        compiler_params=_cparams("parallel", "parallel"),
    )(nq, nkv, cosf, sin_a, sin_b, pad(q_norm_w.reshape(1, HEAD_DIM)), pad(k_norm_w))


def _nsa_cmp_kernel(kc_ref, vc_ref, pos_ref, w_ref, knw_ref, ko_ref, vo_ref):
    nseg = ko_ref.shape[0]
    half = CMP_LEN // 2
    for which, (src, dst) in enumerate(((kc_ref, ko_ref), (vc_ref, vo_ref))):
        f1 = jnp.zeros((nseg, SLOT), F32)
        f2 = jnp.zeros((nseg, SLOT), F32)
        for p in range(half):
            rows = src[pl.ds(p, nseg, stride=CMP_STRIDE), :]
            f1 = f1 + _dot(rows + pos_ref[which, p:p + 1, :], w_ref[which, p])
            f2 = f2 + _dot(rows + pos_ref[which, half + p:half + p + 1, :], w_ref[which, half + p])
        pre = f1 + pltpu.roll(f2, nseg - 1, 0)
        if which == 0:
            pre = _rms(pre, knw_ref[...], HEAD_DIM)
        dst[...] = pre


def _nsa_cmp(kc, vc, cmp_pos, cmp_w, k_norm_w0, B, S):
    nseg = S // CMP_STRIDE
    pos = jnp.pad(cmp_pos, ((0, 0), (0, 0), (0, SLOT - HEAD_DIM)))
    w = cmp_w.reshape(2, CMP_LEN, HEAD_DIM, HEAD_DIM)
    w = jnp.pad(w, ((0, 0), (0, 0), (0, SLOT - HEAD_DIM), (0, SLOT - HEAD_DIM))).astype(BF16)
    knw = jnp.pad(k_norm_w0.reshape(1, HEAD_DIM), ((0, 0), (0, SLOT - HEAD_DIM)))
    per_b = lambda b: (b, 0)
    return pl.pallas_call(
        _nsa_cmp_kernel,
        grid=(B,),
        in_specs=[pl.BlockSpec((S, SLOT), per_b),
                  pl.BlockSpec((S, SLOT), per_b),
                  pl.BlockSpec((2, CMP_LEN, SLOT), lambda b: (0, 0, 0)),
                  pl.BlockSpec((2, CMP_LEN, SLOT, SLOT), lambda b: (0, 0, 0, 0)),
                  pl.BlockSpec((1, SLOT), lambda b: (0, 0))],
        out_specs=[pl.BlockSpec((nseg, SLOT), per_b), pl.BlockSpec((nseg, SLOT), per_b)],
        out_shape=[jax.ShapeDtypeStruct((B * nseg, SLOT), F32),
                   jax.ShapeDtypeStruct((B * nseg, SLOT), F32)],
        compiler_params=_cparams("parallel"),
    )(kc, vc, pos, w, knw)


def _nsa_attn_kernel(q_ref, kc_ref, vc_ref, ksw_ref, g_ref, ow_ref, o_ref, *, S):
    tq = Q_TILE
    n_cmp_pad = S // CMP_STRIDE
    n_sel = S // SEL_BLOCK
    n_top = min(SEL_TOPK, n_sel)
    n_win = WINDOW + tq
    scale = HEAD_DIM ** -0.5
    qi = pl.program_id(1)
    t0 = qi * tq

    t_c = t0 + _iota((tq, n_cmp_pad), 0)
    c_c = _iota((tq, n_cmp_pad), 1)
    vis = (c_c * CMP_STRIDE + (CMP_LEN - 1)) <= t_c
    kc = kc_ref[...]
    vc = vc_ref[...]
    qs = [q_ref[:, h * SLOT:(h + 1) * SLOT] for h in range(NSA_HEADS)]
    o_cmp = []
    p_sum = jnp.zeros((tq, n_cmp_pad), F32)
    for h in range(NSA_HEADS):
        s = jnp.where(vis, _dot_nt(qs[h], kc) * scale, NEG_BIG)
        p = jnp.where(vis, _softmax_rows(s), 0.0)
        o_cmp.append(_dot(p, vc))
        p_sum = p_sum + p

    r_o = _iota((n_cmp_pad, LANE), 0) * CMP_STRIDE
    j_o = _iota((n_cmp_pad, LANE), 1) * SEL_BLOCK
    overlap = ((r_o < j_o + SEL_BLOCK) & (r_o + CMP_LEN > j_o)
               & (_iota((n_cmp_pad, LANE), 0) < n_cmp_pad - 1)).astype(BF16)
    imp_t = _dot_split_lhs(p_sum, overlap).T
    j_t = _iota((LANE, tq), 0)
    t_t = t0 + _iota((LANE, tq), 1)
    blk = t_t // SEL_BLOCK
    forced = (j_t == 0) | (j_t == blk) | (j_t == blk - 1)
    causal = (j_t * SEL_BLOCK <= t_t) & (j_t < n_sel)
    val = jnp.where(causal, jnp.where(forced, FORCE_SCORE, imp_t), -jnp.inf)
    rank = jnp.zeros((LANE, tq), jnp.int32)
    for jp in range(n_sel):
        vj = val[jp:jp + 1, :]
        beats = (vj > val) | ((vj == val) & (j_t > jp))
        rank = rank + beats.astype(jnp.int32)
    sel_t = (causal & (rank < n_top)).astype(F32)
    sel = sel_t.T.astype(BF16)
    expand = (_iota((LANE, S), 0) == _iota((LANE, S), 1) // SEL_BLOCK).astype(BF16)
    sel_keys = jnp.dot(sel, expand, preferred_element_type=F32)

    t_s = t0 + _iota((tq, S), 0)
    k_s = _iota((tq, S), 1)
    ok_sel = (sel_keys > 0.5) & (k_s <= t_s)
    k_sel = ksw_ref[:, 0:SLOT]
    v_sel = ksw_ref[:, SLOT:2 * SLOT]
    o_sel = []
    for h in range(NSA_HEADS):
        s = jnp.where(ok_sel, _dot_nt(qs[h], k_sel) * scale, NEG_BIG)
        o_sel.append(_dot(_softmax_rows(s), v_sel))

    start = pl.multiple_of(jnp.clip(t0 - WINDOW, 0, S - n_win), tq)
    k_win = ksw_ref[pl.ds(start, n_win), 2 * SLOT:3 * SLOT]
    v_win = ksw_ref[pl.ds(start, n_win), 3 * SLOT:4 * SLOT]
    rel = (t0 + _iota((tq, n_win), 0)) - (start + _iota((tq, n_win), 1))
    ok_win = (rel >= 0) & (rel < WINDOW)
    o_win = []
    for h in range(NSA_HEADS):
        s = jnp.where(ok_win, _dot_nt(qs[h], k_win) * scale, NEG_BIG)
        o_win.append(_dot(_softmax_rows(s), v_win))

    gate = _sigmoid(g_ref[...])
    n_g = 3 * NSA_HEADS
    e_g = (_iota((LANE, n_g * SLOT), 0) == _iota((LANE, n_g * SLOT), 1) // SLOT).astype(BF16)
    g_e = _dot_split_lhs(gate, e_g)
    outs = []
    for h in range(NSA_HEADS):
        g0 = g_e[:, (3 * h) * SLOT:(3 * h + 1) * SLOT]
        g1 = g_e[:, (3 * h + 1) * SLOT:(3 * h + 2) * SLOT]
        g2 = g_e[:, (3 * h + 2) * SLOT:(3 * h + 3) * SLOT]
        outs.append(g0 * o_cmp[h] + g1 * o_sel[h] + g2 * o_win[h])
    o = jnp.concatenate(outs, axis=1)
    o_ref[...] = _rms(o, ow_ref[...], NSA_WIDTH)


def _nsa_attn(q_r, k_cmp, v_cmp, ksw, gate, out_w, B, S):
    T = B * S
    nq = S // Q_TILE
    nseg = S // CMP_STRIDE
    row = lambda b, i: (b * nq + i, 0)
    per_b = lambda b, i: (b, 0)
    return pl.pallas_call(
        functools.partial(_nsa_attn_kernel, S=S),
        grid=(B, nq),
        in_specs=[pl.BlockSpec((Q_TILE, NSA_HEADS * SLOT), row),
                  pl.BlockSpec((nseg, SLOT), per_b),
                  pl.BlockSpec((nseg, SLOT), per_b),
                  pl.BlockSpec((S, 4 * SLOT), per_b),
                  pl.BlockSpec((Q_TILE, LANE), row),
                  pl.BlockSpec((1, NSA_HEADS * SLOT), lambda b, i: (0, 0))],
        out_specs=pl.BlockSpec((Q_TILE, NSA_HEADS * SLOT), row),
        out_shape=jax.ShapeDtypeStruct((T, NSA_HEADS * SLOT), F32),
        compiler_params=_cparams("parallel", "parallel"),
    )(q_r, k_cmp, v_cmp, ksw, gate, _pad_heads_vec(out_w, NSA_HEADS))


def _outproj_kernel(x_ref, yssd_ref, ysb_ref, ynsa_ref, sbw_ref, w_ref, o_ref):
    ysb = _rms(ysb_ref[...], sbw_ref[...], SB_WIDTH)
    w0 = SSD_WIDTH
    w1 = w0 + SB_HEADS * SLOT
    w2 = w1 + NSA_HEADS * SLOT
    acc = x_ref[...]
    acc = acc + jnp.dot(yssd_ref[...].astype(BF16), w_ref[0:w0, :], preferred_element_type=F32)
    acc = acc + jnp.dot(ysb.astype(BF16), w_ref[w0:w1, :], preferred_element_type=F32)
    acc = acc + jnp.dot(ynsa_ref[...].astype(BF16), w_ref[w1:w2, :], preferred_element_type=F32)
    o_ref[...] = acc


def _outproj_rowmap():
    src = list(range(SSD_WIDTH))
    for base, nheads in ((SSD_WIDTH, SB_HEADS), (SSD_WIDTH + SB_WIDTH, NSA_HEADS)):
        for h in range(nheads):
            src.extend(range(base + h * HEAD_DIM, base + (h + 1) * HEAD_DIM))
            src.extend([-1] * (SLOT - HEAD_DIM))
    return np.asarray(src, np.int32)


_OUT_ROWMAP = _outproj_rowmap()


def _outproj(xt, y_ssd, y_sb, y_nsa, sb_out_w, w_pad):
    T = xt.shape[0]
    tm = 256
    kdim = _OUT_ROWMAP.shape[0]
    row = lambda i: (i, 0)
    fixed = lambda i: (0, 0)
    return pl.pallas_call(
        _outproj_kernel,
        grid=(T // tm,),
        in_specs=[pl.BlockSpec((tm, D_MODEL), row),
                  pl.BlockSpec((tm, SSD_WIDTH), row),
                  pl.BlockSpec((tm, SB_HEADS * SLOT), row),
                  pl.BlockSpec((tm, NSA_HEADS * SLOT), row),
                  pl.BlockSpec((1, SB_HEADS * SLOT), fixed),
                  pl.BlockSpec((kdim, D_MODEL), fixed)],
        out_specs=pl.BlockSpec((tm, D_MODEL), row),
        out_shape=jax.ShapeDtypeStruct((T, D_MODEL), F32),
        compiler_params=_cparams("parallel"),
    )(xt, y_ssd, y_sb, y_nsa, _pad_heads_vec(sb_out_w, SB_HEADS), w_pad)


def _topk_vals(s, k):
    n = s.shape[0]
    io = _iota(s.shape, 0)
    ko = _iota((k, s.shape[1]), 0)
    out = jnp.zeros((k, s.shape[1]), F32)
    for r in range(k):
        m = jnp.max(s, axis=0, keepdims=True)
        out = jnp.where(ko == r, m, out)
        if r + 1 < k:
            first = jnp.min(jnp.where(s == m, io, n), axis=0, keepdims=True)
            s = jnp.where(io == first, -jnp.inf, s)
    return out


def _peer_select_kernel(x_ref, nw_ref, wq1_ref, wq2_ref, sk_ref,
                        h_ref, th_ref, e1_ref, s2_ref, e2_ref):
    K = PEER_TOPK
    half = PEER_QDIM // 2
    h = _rms(x_ref[...], nw_ref[...], D_MODEL)
    h1 = h.astype(BF16)
    h2 = (h - h1.astype(F32)).astype(BF16)
    h_ref[...] = h1
    nt = (((1,), (1,)), ((), ()))
    wq1 = wq1_ref[...]
    q_t = (lax.dot_general(wq1, h1, nt, preferred_element_type=F32)
           + lax.dot_general(wq1, h2, nt, preferred_element_type=F32)
           + lax.dot_general(wq2_ref[...], h1, nt, preferred_element_type=F32))
    for hd in range(PEER_HEADS):
        scores, tops = [], []
        for p in range(2):
            r0 = (hd * 2 + p) * half
            s = _dot_hi(sk_ref[p], q_t[r0:r0 + half, :])
            scores.append(s)
            tops.append(_topk_vals(s, K))
        sv1, sv2 = tops
        cand = jnp.concatenate([sv1[a:a + 1, :] + sv2 for a in range(K)], axis=0)
        cv = _topk_vals(cand, K)
        tau = cv[K - 1:K, :]
        zsum = jnp.sum(jnp.exp(cv - cv[0:1, :]), axis=0, keepdims=True)
        s1, s2 = scores
        theta = jnp.full(s1.shape, jnp.inf, F32)
        for b in range(K):
            vb = sv2[b:b + 1, :]
            theta = jnp.where(s1 + vb >= tau, vb, theta)
        th_ref[hd] = theta
        e1_ref[hd] = jnp.exp(s1 - sv1[0:1, :])
        s2_ref[hd] = s2
        e2_ref[hd] = jnp.exp(s2 - sv2[0:1, :]) / zsum


def _peer_dense_kernel(h_ref, x_ref, u_ref, vt_ref, th_ref, e1_ref, s2_ref, e2_ref,
                       o_ref, acc_ref, p_ref, *, chunk):
    c = pl.program_id(1)
    nk = N_KEYS

    @pl.when(c == 0)
    def _():
        acc_ref[...] = jnp.zeros_like(acc_ref)

    nt = (((1,), (1,)), ((), ()))
    h_t = lax.dot_general(u_ref[...], h_ref[...], nt, preferred_element_type=F32)
    for il in range(chunk):
        i1 = c * chunk + il
        a = h_t[il * nk:(il + 1) * nk, :]
        act = 0.5 * a * (1.0 + lax.erf(a * (2.0 ** -0.5)))
        w = jnp.zeros(a.shape, F32)
        for hd in range(PEER_HEADS):
            th = th_ref[hd, pl.ds(i1, 1), :]
            e1 = e1_ref[hd, pl.ds(i1, 1), :]
            w = w + jnp.where(s2_ref[hd] >= th, e2_ref[hd], 0.0) * e1
        p_ref[il * nk:(il + 1) * nk, :] = (w * act).astype(BF16)
    acc_ref[...] += jnp.dot(vt_ref[...], p_ref[...], preferred_element_type=F32)

    @pl.when(c == pl.num_programs(1) - 1)
    def _():
        o_ref[...] = x_ref[...] + acc_ref[...].T


def _peer(x1, norm_w, wq_t1, wq_t2, sub_keys, u_bf, vt_bf):
    T = x1.shape[0]
    H, nk = PEER_HEADS, N_KEYS
    tb_s = 256
    row = lambda i: (i, 0)
    fixed2 = lambda i: (0, 0)
    tok3 = lambda i: (0, 0, i)
    head_shape = jax.ShapeDtypeStruct((H, nk, T), F32)
    h_bf, theta, e1, s2, e2 = pl.pallas_call(
        _peer_select_kernel,
        grid=(T // tb_s,),
        in_specs=[pl.BlockSpec((tb_s, D_MODEL), row),
                  pl.BlockSpec((1, D_MODEL), fixed2),
                  pl.BlockSpec((H * PEER_QDIM, D_MODEL), fixed2),
                  pl.BlockSpec((H * PEER_QDIM, D_MODEL), fixed2),
                  pl.BlockSpec((2, nk, PEER_QDIM // 2), lambda i: (0, 0, 0))],
        out_specs=[pl.BlockSpec((tb_s, D_MODEL), row)] + [pl.BlockSpec((H, nk, tb_s), tok3)] * 4,
        out_shape=[jax.ShapeDtypeStruct((T, D_MODEL), BF16)] + [head_shape] * 4,
        compiler_params=_cparams("parallel"),
    )(x1, norm_w.reshape(1, D_MODEL), wq_t1, wq_t2, sub_keys)

    tb = 512
    chunk = 8
    n_exp = nk * nk
    tok = lambda i, c: (i, 0)
    tok3d = lambda i, c: (0, 0, i)
    return pl.pallas_call(
        functools.partial(_peer_dense_kernel, chunk=chunk),
        grid=(T // tb, nk // chunk),
        in_specs=[pl.BlockSpec((tb, D_MODEL), tok),
                  pl.BlockSpec((tb, D_MODEL), tok),
                  pl.BlockSpec((chunk * nk, D_MODEL), lambda i, c: (c, 0)),
                  pl.BlockSpec((D_MODEL, chunk * nk), lambda i, c: (0, c))]
                 + [pl.BlockSpec((H, nk, tb), tok3d)] * 4,
        out_specs=pl.BlockSpec((tb, D_MODEL), tok),
        out_shape=jax.ShapeDtypeStruct((T, D_MODEL), F32),
        scratch_shapes=[pltpu.VMEM((D_MODEL, tb), F32),
                        pltpu.VMEM((chunk * nk, tb), BF16)],
        compiler_params=_cparams("parallel", "arbitrary"),
    )(h_bf, x1, u_bf, vt_bf, theta, e1, s2, e2)


def kernel(x, norm1_w, w_in, conv_w, conv_b, dt_bias, a_log, d_skip, ssd_norm_w, sb_out_w,
           nsa_q_norm_w, nsa_k_norm_w, nsa_cmp_pos, nsa_cmp_w, nsa_out_w, w_out, norm2_w,
           peer_w_q, peer_sub_keys, peer_u, peer_v):
    B, S, _ = x.shape
    depth = w_in.shape[0]
    assert S == 2048, "NSA tiling is laid out for 2048-token sequences"
    T = B * S
    ropes = _rope_tables(S)
    xt = x.reshape(T, D_MODEL)
    for l in range(depth):
        w_in_pad = _pad_cols(w_in[l], _IN_COLMAP).astype(BF16)
        z, xbc, dt, sbq, sbk, sbv, nq, nkv, gate = _inproj(xt, norm1_w[l], w_in_pad)
        y_ssd = _ssd(z, xbc, dt, conv_w[l], conv_b[l], dt_bias[l], a_log[l], d_skip[l], ssd_norm_w[l], B, S)
        y_sb = _sb(sbq, sbk, sbv, B, S)
        q_r, kc, vc, ksw = _nsa_prep(nq, nkv, ropes, nsa_q_norm_w[l], nsa_k_norm_w[l], B, S)
        k_cmp, v_cmp = _nsa_cmp(kc, vc, nsa_cmp_pos[l], nsa_cmp_w[l], nsa_k_norm_w[l, 0], B, S)
        y_nsa = _nsa_attn(q_r, k_cmp, v_cmp, ksw, gate, nsa_out_w[l], B, S)
        w_out_pad = _pad_cols(w_out[l].T, _OUT_ROWMAP).T.astype(BF16)
        x1 = _outproj(xt, y_ssd, y_sb, y_nsa, sb_out_w[l], w_out_pad)
        wq_t = peer_w_q[l].T
        wq_t1 = wq_t.astype(BF16)
        wq_t2 = (wq_t - wq_t1.astype(F32)).astype(BF16)
        xt = _peer(x1, norm2_w[l], wq_t1, wq_t2, peer_sub_keys[l],
                   peer_u[l].astype(BF16), peer_v[l].T.astype(BF16))
    return xt.reshape(B, S, D_MODEL)
```

```python
import functools
import math

import numpy as np
import jax
import jax.numpy as jnp
from jax import lax
from jax.experimental import pallas as pl
from jax.experimental.pallas import tpu as pltpu

F32 = jnp.float32
BF16 = jnp.bfloat16

D_MODEL = 1024
HEAD_DIM = 64
SSD_WIDTH = 512
SSD_HEADS = 8
SSD_GROUPS = 2
SSD_STATE = 128
SSD_CONV = 4
SSD_CHUNK = 128
SSD_CONV_DIM = SSD_WIDTH + 2 * SSD_GROUPS * SSD_STATE
SB_WIDTH = 256
SB_HEADS = 4
NSA_WIDTH = 256
NSA_HEADS = 4
CMP_LEN = 32
CMP_STRIDE = 16
SEL_BLOCK = 64
SEL_TOPK = 16
WINDOW = 512
FORCE_SCORE = 1e9
ROT_DIM = 16
ROPE_THETA = 500000.0
PEER_HEADS = 8
N_KEYS = 128
PEER_TOPK = 16
PEER_QDIM = 256
NORM_EPS = 1e-6
NEG_BIG = -1e30

LANE = 128
SUBLANE = 8
V7X_VMEM_BYTES = 64 * 1024 * 1024
VMEM_LIMIT = V7X_VMEM_BYTES - 12 * 1024 * 1024

SLOT = LANE
Q_TILE = 128


def _cparams(*sem):
    return pltpu.CompilerParams(dimension_semantics=sem, vmem_limit_bytes=VMEM_LIMIT)


def _iota(shape, dim):
    return lax.broadcasted_iota(jnp.int32, shape, dim)


def _dot(a, b):
    return jnp.dot(a.astype(BF16), b.astype(BF16), preferred_element_type=F32)


def _dot_nt(a, b):
    return lax.dot_general(a.astype(BF16), b.astype(BF16), (((1,), (1,)), ((), ())),
                           preferred_element_type=F32)


def _split(a, n):
    parts, r = [], a
    for i in range(n):
        p = r.astype(BF16)
        parts.append(p)
        if i + 1 < n:
            r = r - p.astype(F32)
    return parts


def _dot_split_lhs(a, b_exact, n=3):
    acc = None
    for p in _split(a, n):
        t = jnp.dot(p, b_exact, preferred_element_type=F32)
        acc = t if acc is None else acc + t
    return acc


def _dot_split_rhs(a_exact, b, n=3):
    acc = None
    for p in _split(b, n):
        t = jnp.dot(a_exact, p, preferred_element_type=F32)
        acc = t if acc is None else acc + t
    return acc


def _dot_hi(a, b):
    a1, a2 = _split(a, 2)
    b1, b2 = _split(b, 2)
    return (jnp.dot(a1, b1, preferred_element_type=F32) + jnp.dot(a1, b2, preferred_element_type=F32)
            + jnp.dot(a2, b1, preferred_element_type=F32))


def _sigmoid(x):
    return 1.0 / (1.0 + jnp.exp(-x))


def _softplus(x):
    return jnp.maximum(x, 0.0) + jnp.log1p(jnp.exp(-jnp.abs(x)))


def _rms(x, w, n):
    ms = jnp.sum(x * x, axis=-1, keepdims=True) * (1.0 / n)
    return x * lax.rsqrt(ms + NORM_EPS) * w


def _softmax_rows(s):
    m = jnp.max(s, axis=-1, keepdims=True)
    e = jnp.exp(s - m)
    return e / jnp.sum(e, axis=-1, keepdims=True)


_IN_SEGS = (("z", 512, F32), ("xbc", 1024, F32), ("dt", 128, F32),
            ("sbq", 512, BF16), ("sbk", 512, BF16), ("sbv", 512, BF16),
            ("nq", 512, F32), ("nkv", 768, F32), ("gate", 128, F32))
_IN_TOTAL = sum(s[1] for s in _IN_SEGS)


def _inproj_colmap():
    src = []

    def real(a, n, pad_to=None):
        src.extend(range(a, a + n))
        if pad_to:
            src.extend([-1] * (pad_to - n))

    def heads(a, nheads):
        for h in range(nheads):
            src.extend(range(a + h * HEAD_DIM, a + (h + 1) * HEAD_DIM))
            src.extend([-1] * (SLOT - HEAD_DIM))

    o_xbc = SSD_WIDTH
    o_dt = o_xbc + SSD_CONV_DIM
    o_sb = o_dt + SSD_HEADS
    o_nq = o_sb + 3 * SB_WIDTH
    o_nkv = o_nq + NSA_WIDTH
    o_gate = o_nkv + 6 * HEAD_DIM
    real(0, SSD_WIDTH)
    real(o_xbc, SSD_CONV_DIM)
    real(o_dt, SSD_HEADS, LANE)
    heads(o_sb, SB_HEADS)
    heads(o_sb + SB_WIDTH, SB_HEADS)
    heads(o_sb + 2 * SB_WIDTH, SB_HEADS)
    heads(o_nq, NSA_HEADS)
    heads(o_nkv, 6)
    real(o_gate, 3 * NSA_HEADS, LANE)
    src = np.asarray(src, np.int32)
    assert src.shape[0] == _IN_TOTAL
    return src


_IN_COLMAP = _inproj_colmap()


def _pad_cols(w, colmap):
    g = jnp.take(w, jnp.asarray(np.maximum(colmap, 0)), axis=-1)
    return jnp.where(jnp.asarray(colmap >= 0), g, 0.0)


def _pad_heads_vec(v, nheads):
    v = v.reshape(nheads, HEAD_DIM)
    v = jnp.pad(v, ((0, 0), (0, SLOT - HEAD_DIM)))
    return v.reshape(1, nheads * SLOT)


def _inproj_kernel(x_ref, nw_ref, w_ref, *out_refs):
    x = x_ref[...]
    h = _rms(x, nw_ref[...], D_MODEL).astype(BF16)
    off = 0
    for (name, width, dt), o_ref in zip(_IN_SEGS, out_refs):
        o_ref[...] = jnp.dot(h, w_ref[:, off:off + width], preferred_element_type=F32).astype(dt)
        off += width


def _inproj(xt, norm_w, w_pad):
    T = xt.shape[0]
    tm = 256
    out_shape = [jax.ShapeDtypeStruct((T, w), dt) for (_, w, dt) in _IN_SEGS]
    out_specs = [pl.BlockSpec((tm, w), lambda i: (i, 0)) for (_, w, _) in _IN_SEGS]
    return pl.pallas_call(
        _inproj_kernel,
        grid=(T // tm,),
        in_specs=[pl.BlockSpec((tm, D_MODEL), lambda i: (i, 0)),
                  pl.BlockSpec((1, D_MODEL), lambda i: (0, 0)),
                  pl.BlockSpec((D_MODEL, _IN_TOTAL), lambda i: (0, 0))],
        out_specs=out_specs,
        out_shape=out_shape,
        compiler_params=_cparams("parallel"),
    )(xt, norm_w.reshape(1, D_MODEL), w_pad)


def _ssd_kernel(z_ref, xbc_ref, dt_ref, cw_ref, cb_ref, dtb_ref, alog_ref, dsk_ref, nw_ref,
                o_ref, xbuf, state):
    L = SSD_CHUNK
    c = pl.program_id(1)

    @pl.when(c == 0)
    def _():
        xbuf[0:SUBLANE, :] = jnp.zeros((SUBLANE, SSD_CONV_DIM), F32)
        state[...] = jnp.zeros_like(state)

    xcur = xbc_ref[...]
    xbuf[SUBLANE:SUBLANE + L, :] = xcur
    acc = jnp.broadcast_to(cb_ref[...], (L, SSD_CONV_DIM))
    for k in range(SSD_CONV):
        s0 = SUBLANE - (SSD_CONV - 1) + k
        acc = acc + xbuf[s0:s0 + L, :] * cw_ref[k:k + 1, :]
    xbuf[0:SUBLANE, :] = xcur[L - SUBLANE:L, :]
    xc = acc * _sigmoid(acc)

    xs = xc[:, 0:SSD_WIDTH]
    dt = _softplus(dt_ref[...] + dtb_ref[...])
    a_dec = dt * (-jnp.exp(alog_ref[...]))

    row = _iota((L, L), 0)
    col = _iota((L, L), 1)
    tril = row >= col
    tri = tril.astype(BF16)
    cs = _dot_split_rhs(tri, a_dec)
    e64 = (_iota((LANE, SSD_WIDTH), 0) == _iota((LANE, SSD_WIDTH), 1) // HEAD_DIM).astype(BF16)
    e128 = (_iota((LANE, SSD_HEADS * LANE), 0) == _iota((LANE, SSD_HEADS * LANE), 1) // LANE).astype(BF16)
    dt_e = _dot_split_lhs(dt, e64)
    cs_e = _dot_split_lhs(cs, e64)
    cs_e2 = _dot_split_lhs(cs, e128)
    cs_t = cs.T
    cs_last = cs_e[L - 1:L, :]
    x_dt = xs * dt_e
    x_dec = x_dt * jnp.exp(cs_last - cs_e)
    in_dec = jnp.exp(cs_e)
    chunk_dec = jnp.exp(cs_last)
    lane_lo = _iota((L, LANE), 1) < HEAD_DIM

    ys = []
    for g in range(SSD_GROUPS):
        b_g = xc[:, SSD_WIDTH + g * SSD_STATE:SSD_WIDTH + (g + 1) * SSD_STATE]
        c_g = xc[:, SSD_WIDTH + (SSD_GROUPS + g) * SSD_STATE:SSD_WIDTH + (SSD_GROUPS + g + 1) * SSD_STATE]
        cb = _dot_nt(c_g, b_g)
        b_gt = b_g.T
        for m in range(2 * g, 2 * g + 2):
            lanes = slice(m * LANE, (m + 1) * LANE)
            xp = x_dt[:, lanes]
            yd = []
            for h in (2 * m, 2 * m + 1):
                diff = cs_e2[:, h * LANE:(h + 1) * LANE] - cs_t[h:h + 1, :]
                lm = jnp.exp(jnp.where(tril, diff, -jnp.inf))
                yd.append(_dot(lm * cb, xp))
            y_diag = jnp.where(lane_lo, yd[0], yd[1])
            s_prev = state[m]
            y_off = _dot(c_g, s_prev) * in_dec[:, lanes]
            state[m] = s_prev * chunk_dec[:, lanes] + _dot(b_gt, x_dec[:, lanes])
            ys.append(y_diag + y_off + xs[:, lanes] * dsk_ref[:, lanes])
    y = jnp.concatenate(ys, axis=1)
    zt = z_ref[...]
    o_ref[...] = _rms(y * (zt * _sigmoid(zt)), nw_ref[...], SSD_WIDTH)


def _ssd(z, xbc, dt, conv_w, conv_b, dt_bias, a_log, d_skip, norm_w, B, S):
    T = B * S
    L = SSD_CHUNK
    nc = S // L
    pad8 = lambda v: jnp.pad(v, (0, LANE - SSD_HEADS)).reshape(1, LANE)
    row = lambda b, c: (b * nc + c, 0)
    fixed = lambda b, c: (0, 0)
    return pl.pallas_call(
        _ssd_kernel,
        grid=(B, nc),
        in_specs=[pl.BlockSpec((L, SSD_WIDTH), row),
                  pl.BlockSpec((L, SSD_CONV_DIM), row),
                  pl.BlockSpec((L, LANE), row),
                  pl.BlockSpec((SSD_CONV, SSD_CONV_DIM), fixed),
                  pl.BlockSpec((1, SSD_CONV_DIM), fixed),
                  pl.BlockSpec((1, LANE), fixed),
                  pl.BlockSpec((1, LANE), fixed),
                  pl.BlockSpec((1, SSD_WIDTH), fixed),
                  pl.BlockSpec((1, SSD_WIDTH), fixed)],
        out_specs=pl.BlockSpec((L, SSD_WIDTH), row),
        out_shape=jax.ShapeDtypeStruct((T, SSD_WIDTH), F32),
        scratch_shapes=[pltpu.VMEM((SUBLANE + L, SSD_CONV_DIM), F32),
                        pltpu.VMEM((SSD_HEADS // 2, SSD_STATE, LANE), F32)],
        compiler_params=_cparams("parallel", "arbitrary"),
    )(z, xbc, dt, conv_w, conv_b.reshape(1, -1), pad8(dt_bias), pad8(a_log),
      jnp.repeat(d_skip, HEAD_DIM).reshape(1, SSD_WIDTH), norm_w.reshape(1, SSD_WIDTH))


SB_LOG_CUTOFF = -104.0


def _sb_kernel(q_ref, k_ref, v_ref, o_ref, cf_ref):
    tq = Q_TILE
    H = SB_HEADS
    qi = pl.program_id(1)
    row = _iota((tq, tq), 0)
    col = _iota((tq, tq), 1)
    strict = col < row
    r2 = _iota((tq, 2 * tq), 0)
    c2 = _iota((tq, 2 * tq), 1)
    umat = ((r2 > c2) | (c2 >= tq)).astype(BF16)
    scale = HEAD_DIM ** -0.5
    o_ref[...] = jnp.zeros_like(o_ref)
    cf_ref[...] = jnp.zeros_like(cf_ref)

    def cond(c):
        i, live = c
        return jnp.logical_and(i <= qi, live > 0)

    def body(c):
        i, _ = c
        ks = pl.multiple_of((qi - i) * tq, tq)
        mask = jnp.logical_or(i > 0, strict)
        cf_max = None
        for h in range(H):
            lanes = slice(h * SLOT, (h + 1) * SLOT)
            kt = k_ref[pl.ds(ks, tq), lanes]
            vt = v_ref[pl.ds(ks, tq), lanes]
            z = _dot_nt(q_ref[:, lanes], kt) * scale
            l1p = jnp.log1p(jnp.exp(-jnp.abs(z)))
            log_beta = jnp.where(mask, jnp.minimum(z, 0.0) - l1p, -jnp.inf)
            log_stay = jnp.where(mask, -jnp.maximum(z, 0.0) - l1p, 0.0)
            r = _dot_split_lhs(log_stay, umat, 2)
            cf = cf_ref[h]
            w = jnp.exp(log_beta + r[:, 0:tq] + cf)
            o_ref[:, lanes] += _dot(w, vt)
            cf = cf + r[:, tq:2 * tq]
            cf_ref[h] = cf
            cf_max = cf if cf_max is None else jnp.maximum(cf_max, cf)
        live = (jnp.max(cf_max) > SB_LOG_CUTOFF).astype(jnp.int32)
        return i + 1, live

    lax.while_loop(cond, body, (jnp.int32(0), jnp.int32(1)))


def _sb(sbq, sbk, sbv, B, S):
    T = B * S
    nq = S // Q_TILE
    W = SB_HEADS * SLOT
    return pl.pallas_call(
        _sb_kernel,
        grid=(B, nq),
        in_specs=[pl.BlockSpec((Q_TILE, W), lambda b, i: (b * nq + i, 0)),
                  pl.BlockSpec((S, W), lambda b, i: (b, 0)),
                  pl.BlockSpec((S, W), lambda b, i: (b, 0))],
        out_specs=pl.BlockSpec((Q_TILE, W), lambda b, i: (b * nq + i, 0)),
        out_shape=jax.ShapeDtypeStruct((T, W), F32),
        scratch_shapes=[pltpu.VMEM((SB_HEADS, Q_TILE, SLOT), F32)],
        compiler_params=_cparams("parallel", "arbitrary"),
    )(sbq, sbk, sbv)


def _rope_tables(S):
    pos = jnp.arange(S, dtype=F32)
    inv_freq = ROPE_THETA ** (-jnp.arange(0, ROT_DIM, 2, dtype=F32) / ROT_DIM)
    ang = pos[:, None] * inv_freq[None, :]
    cos, sin = jnp.cos(ang), jnp.sin(ang)
    half = ROT_DIM // 2
    ones = jnp.ones((S, SLOT - ROT_DIM), F32)
    zeros_h = jnp.zeros((S, half), F32)
    zeros_r = jnp.zeros((S, SLOT - ROT_DIM), F32)
    cosf = jnp.concatenate([cos, cos, ones], axis=1)
    sin_a = jnp.concatenate([-sin, zeros_h, zeros_r], axis=1)
    sin_b = jnp.concatenate([zeros_h, sin, zeros_r], axis=1)
    return cosf, sin_a, sin_b


def _nsa_prep_kernel(q_ref, kv_ref, cos_ref, sa_ref, sb_ref, qw_ref, kw_ref,
                     qo_ref, kc_ref, vc_ref, ksw_ref):
    cos, sa, sb = cos_ref[...], sa_ref[...], sb_ref[...]
    half = ROT_DIM // 2

    def rope(x):
        return x * cos + pltpu.roll(x, SLOT - half, 1) * sa + pltpu.roll(x, half, 1) * sb

    qw = qw_ref[...]
    for h in range(NSA_HEADS):
        lanes = slice(h * SLOT, (h + 1) * SLOT)
        qo_ref[:, lanes] = rope(_rms(q_ref[:, lanes], qw, HEAD_DIM)).astype(BF16)
    kc_ref[...] = rope(kv_ref[:, 0:SLOT])
    vc_ref[...] = kv_ref[:, SLOT:2 * SLOT]
    ksw_ref[:, 0:SLOT] = rope(_rms(kv_ref[:, 2 * SLOT:3 * SLOT], kw_ref[1:2, :], HEAD_DIM)).astype(BF16)
    ksw_ref[:, SLOT:2 * SLOT] = kv_ref[:, 3 * SLOT:4 * SLOT].astype(BF16)
    ksw_ref[:, 2 * SLOT:3 * SLOT] = rope(_rms(kv_ref[:, 4 * SLOT:5 * SLOT], kw_ref[2:3, :], HEAD_DIM)).astype(BF16)
    ksw_ref[:, 3 * SLOT:4 * SLOT] = kv_ref[:, 5 * SLOT:6 * SLOT].astype(BF16)


def _nsa_prep(nq, nkv, ropes, q_norm_w, k_norm_w, B, S):
    T = B * S
    ts = 256
    nt = S // ts
    cosf, sin_a, sin_b = ropes
    pad = lambda v: jnp.pad(v, ((0, 0), (0, SLOT - HEAD_DIM)))
    row = lambda b, i: (b * nt + i, 0)
    pos = lambda b, i: (i, 0)
    fixed = lambda b, i: (0, 0)
    return pl.pallas_call(
        _nsa_prep_kernel,
        grid=(B, nt),
        in_specs=[pl.BlockSpec((ts, NSA_HEADS * SLOT), row),
                  pl.BlockSpec((ts, 6 * SLOT), row),
                  pl.BlockSpec((ts, SLOT), pos),
                  pl.BlockSpec((ts, SLOT), pos),
                  pl.BlockSpec((ts, SLOT), pos),
                  pl.BlockSpec((1, SLOT), fixed),
                  pl.BlockSpec((3, SLOT), fixed)],
        out_specs=[pl.BlockSpec((ts, NSA_HEADS * SLOT), row),
                   pl.BlockSpec((ts, SLOT), row),
                   pl.BlockSpec((ts, SLOT), row),
                   pl.BlockSpec((ts, 4 * SLOT), row)],
        out_shape=[jax.ShapeDtypeStruct((T, NSA_HEADS * SLOT), BF16),
                   jax.ShapeDtypeStruct((T, SLOT), F32),
                   jax.ShapeDtypeStruct((T, SLOT), F32),
                   jax.ShapeDtypeStruct((T, 4 * SLOT), BF16)],
        compiler_params=_cparams("parallel", "parallel"),
    )(nq, nkv, cosf, sin_a, sin_b, pad(q_norm_w.reshape(1, HEAD_DIM)), pad(k_norm_w))


def _nsa_cmp_kernel(kc_ref, vc_ref, pos_ref, w_ref, knw_ref, ko_ref, vo_ref):
    nseg = ko_ref.shape[0]
    half = CMP_LEN // 2
    for which, (src, dst) in enumerate(((kc_ref, ko_ref), (vc_ref, vo_ref))):
        f1 = jnp.zeros((nseg, SLOT), F32)
        f2 = jnp.zeros((nseg, SLOT), F32)
        for p in range(half):
            rows = src[pl.ds(p, nseg, stride=CMP_STRIDE), :]
            f1 = f1 + _dot(rows + pos_ref[which, p:p + 1, :], w_ref[which, p])
            f2 = f2 + _dot(rows + pos_ref[which, half + p:half + p + 1, :], w_ref[which, half + p])
        pre = f1 + pltpu.roll(f2, nseg - 1, 0)
        if which == 0:
            pre = _rms(pre, knw_ref[...], HEAD_DIM)
        dst[...] = pre


def _nsa_cmp(kc, vc, cmp_pos, cmp_w, k_norm_w0, B, S):
    nseg = S // CMP_STRIDE
    pos = jnp.pad(cmp_pos, ((0, 0), (0, 0), (0, SLOT - HEAD_DIM)))
    w = cmp_w.reshape(2, CMP_LEN, HEAD_DIM, HEAD_DIM)
    w = jnp.pad(w, ((0, 0), (0, 0), (0, SLOT - HEAD_DIM), (0, SLOT - HEAD_DIM))).astype(BF16)
    knw = jnp.pad(k_norm_w0.reshape(1, HEAD_DIM), ((0, 0), (0, SLOT - HEAD_DIM)))
    per_b = lambda b: (b, 0)
    return pl.pallas_call(
        _nsa_cmp_kernel,
        grid=(B,),
        in_specs=[pl.BlockSpec((S, SLOT), per_b),
                  pl.BlockSpec((S, SLOT), per_b),
                  pl.BlockSpec((2, CMP_LEN, SLOT), lambda b: (0, 0, 0)),
                  pl.BlockSpec((2, CMP_LEN, SLOT, SLOT), lambda b: (0, 0, 0, 0)),
                  pl.BlockSpec((1, SLOT), lambda b: (0, 0))],
        out_specs=[pl.BlockSpec((nseg, SLOT), per_b), pl.BlockSpec((nseg, SLOT), per_b)],
        out_shape=[jax.ShapeDtypeStruct((B * nseg, SLOT), F32),
                   jax.ShapeDtypeStruct((B * nseg, SLOT), F32)],
        compiler_params=_cparams("parallel"),
    )(kc, vc, pos, w, knw)


def _nsa_attn_kernel(q_ref, kc_ref, vc_ref, ksw_ref, g_ref, ow_ref, o_ref, *, S):
    tq = Q_TILE
    n_cmp_pad = S // CMP_STRIDE
    n_sel = S // SEL_BLOCK
    n_top = min(SEL_TOPK, n_sel)
    n_win = WINDOW + tq
    scale = HEAD_DIM ** -0.5
    qi = pl.program_id(1)
    t0 = qi * tq

    t_c = t0 + _iota((tq, n_cmp_pad), 0)
    c_c = _iota((tq, n_cmp_pad), 1)
    vis = (c_c * CMP_STRIDE + (CMP_LEN - 1)) <= t_c
    kc = kc_ref[...]
    vc = vc_ref[...]
    qs = [q_ref[:, h * SLOT:(h + 1) * SLOT] for h in range(NSA_HEADS)]
    o_cmp = []
    p_sum = jnp.zeros((tq, n_cmp_pad), F32)
    for h in range(NSA_HEADS):
        s = jnp.where(vis, _dot_nt(qs[h], kc) * scale, NEG_BIG)
        p = jnp.where(vis, _softmax_rows(s), 0.0)
        o_cmp.append(_dot(p, vc))
        p_sum = p_sum + p

    r_o = _iota((n_cmp_pad, LANE), 0) * CMP_STRIDE
    j_o = _iota((n_cmp_pad, LANE), 1) * SEL_BLOCK
    overlap = ((r_o < j_o + SEL_BLOCK) & (r_o + CMP_LEN > j_o)
               & (_iota((n_cmp_pad, LANE), 0) < n_cmp_pad - 1)).astype(BF16)
    imp_t = _dot_split_lhs(p_sum, overlap).T
    j_t = _iota((LANE, tq), 0)
    t_t = t0 + _iota((LANE, tq), 1)
    blk = t_t // SEL_BLOCK
    forced = (j_t == 0) | (j_t == blk) | (j_t == blk - 1)
    causal = (j_t * SEL_BLOCK <= t_t) & (j_t < n_sel)
    val = jnp.where(causal, jnp.where(forced, FORCE_SCORE, imp_t), -jnp.inf)
    rank = jnp.zeros((LANE, tq), jnp.int32)
    for jp in range(n_sel):
        vj = val[jp:jp + 1, :]
        beats = (vj > val) | ((vj == val) & (j_t > jp))
        rank = rank + beats.astype(jnp.int32)
    sel_t = (causal & (rank < n_top)).astype(F32)
    sel = sel_t.T.astype(BF16)
    expand = (_iota((LANE, S), 0) == _iota((LANE, S), 1) // SEL_BLOCK).astype(BF16)
    sel_keys = jnp.dot(sel, expand, preferred_element_type=F32)

    t_s = t0 + _iota((tq, S), 0)
    k_s = _iota((tq, S), 1)
    ok_sel = (sel_keys > 0.5) & (k_s <= t_s)
    k_sel = ksw_ref[:, 0:SLOT]
    v_sel = ksw_ref[:, SLOT:2 * SLOT]
    o_sel = []
    for h in range(NSA_HEADS):
        s = jnp.where(ok_sel, _dot_nt(qs[h], k_sel) * scale, NEG_BIG)
        o_sel.append(_dot(_softmax_rows(s), v_sel))

    start = pl.multiple_of(jnp.clip(t0 - WINDOW, 0, S - n_win), tq)
    k_win = ksw_ref[pl.ds(start, n_win), 2 * SLOT:3 * SLOT]
    v_win = ksw_ref[pl.ds(start, n_win), 3 * SLOT:4 * SLOT]
    rel = (t0 + _iota((tq, n_win), 0)) - (start + _iota((tq, n_win), 1))
    ok_win = (rel >= 0) & (rel < WINDOW)
    o_win = []
    for h in range(NSA_HEADS):
        s = jnp.where(ok_win, _dot_nt(qs[h], k_win) * scale, NEG_BIG)
        o_win.append(_dot(_softmax_rows(s), v_win))

    gate = _sigmoid(g_ref[...])
    n_g = 3 * NSA_HEADS
    e_g = (_iota((LANE, n_g * SLOT), 0) == _iota((LANE, n_g * SLOT), 1) // SLOT).astype(BF16)
    g_e = _dot_split_lhs(gate, e_g)
    outs = []
    for h in range(NSA_HEADS):
        g0 = g_e[:, (3 * h) * SLOT:(3 * h + 1) * SLOT]
        g1 = g_e[:, (3 * h + 1) * SLOT:(3 * h + 2) * SLOT]
        g2 = g_e[:, (3 * h + 2) * SLOT:(3 * h + 3) * SLOT]
        outs.append(g0 * o_cmp[h] + g1 * o_sel[h] + g2 * o_win[h])
    o = jnp.concatenate(outs, axis=1)
    o_ref[...] = _rms(o, ow_ref[...], NSA_WIDTH)


def _nsa_attn(q_r, k_cmp, v_cmp, ksw, gate, out_w, B, S):
    T = B * S
    nq = S // Q_TILE
    nseg = S // CMP_STRIDE
    row = lambda b, i: (b * nq + i, 0)
    per_b = lambda b, i: (b, 0)
    return pl.pallas_call(
        functools.partial(_nsa_attn_kernel, S=S),
        grid=(B, nq),
        in_specs=[pl.BlockSpec((Q_TILE, NSA_HEADS * SLOT), row),
                  pl.BlockSpec((nseg, SLOT), per_b),
                  pl.BlockSpec((nseg, SLOT), per_b),
                  pl.BlockSpec((S, 4 * SLOT), per_b),
                  pl.BlockSpec((Q_TILE, LANE), row),
                  pl.BlockSpec((1, NSA_HEADS * SLOT), lambda b, i: (0, 0))],
        out_specs=pl.BlockSpec((Q_TILE, NSA_HEADS * SLOT), row),
        out_shape=jax.ShapeDtypeStruct((T, NSA_HEADS * SLOT), F32),
        compiler_params=_cparams("parallel", "parallel"),
    )(q_r, k_cmp, v_cmp, ksw, gate, _pad_heads_vec(out_w, NSA_HEADS))


def _outproj_kernel(x_ref, yssd_ref, ysb_ref, ynsa_ref, sbw_ref, w_ref, o_ref):
    ysb = _rms(ysb_ref[...], sbw_ref[...], SB_WIDTH)
    w0 = SSD_WIDTH
    w1 = w0 + SB_HEADS * SLOT
    w2 = w1 + NSA_HEADS * SLOT
    acc = x_ref[...]
    acc = acc + jnp.dot(yssd_ref[...].astype(BF16), w_ref[0:w0, :], preferred_element_type=F32)
    acc = acc + jnp.dot(ysb.astype(BF16), w_ref[w0:w1, :], preferred_element_type=F32)
    acc = acc + jnp.dot(ynsa_ref[...].astype(BF16), w_ref[w1:w2, :], preferred_element_type=F32)
    o_ref[...] = acc


def _outproj_rowmap():
    src = list(range(SSD_WIDTH))
    for base, nheads in ((SSD_WIDTH, SB_HEADS), (SSD_WIDTH + SB_WIDTH, NSA_HEADS)):
        for h in range(nheads):
            src.extend(range(base + h * HEAD_DIM, base + (h + 1) * HEAD_DIM))
            src.extend([-1] * (SLOT - HEAD_DIM))
    return np.asarray(src, np.int32)


_OUT_ROWMAP = _outproj_rowmap()


def _outproj(xt, y_ssd, y_sb, y_nsa, sb_out_w, w_pad):
    T = xt.shape[0]
    tm = 256
    kdim = _OUT_ROWMAP.shape[0]
    row = lambda i: (i, 0)
    fixed = lambda i: (0, 0)
    return pl.pallas_call(
        _outproj_kernel,
        grid=(T // tm,),
        in_specs=[pl.BlockSpec((tm, D_MODEL), row),
                  pl.BlockSpec((tm, SSD_WIDTH), row),
                  pl.BlockSpec((tm, SB_HEADS * SLOT), row),
                  pl.BlockSpec((tm, NSA_HEADS * SLOT), row),
                  pl.BlockSpec((1, SB_HEADS * SLOT), fixed),
                  pl.BlockSpec((kdim, D_MODEL), fixed)],
        out_specs=pl.BlockSpec((tm, D_MODEL), row),
        out_shape=jax.ShapeDtypeStruct((T, D_MODEL), F32),
        compiler_params=_cparams("parallel"),
    )(xt, y_ssd, y_sb, y_nsa, _pad_heads_vec(sb_out_w, SB_HEADS), w_pad)


def _topk_vals(s, k):
    n = s.shape[0]
    io = _iota(s.shape, 0)
    ko = _iota((k, s.shape[1]), 0)
    out = jnp.zeros((k, s.shape[1]), F32)
    for r in range(k):
        m = jnp.max(s, axis=0, keepdims=True)
        out = jnp.where(ko == r, m, out)
        if r + 1 < k:
            first = jnp.min(jnp.where(s == m, io, n), axis=0, keepdims=True)
            s = jnp.where(io == first, -jnp.inf, s)
    return out


def _peer_select_kernel(x_ref, nw_ref, wq1_ref, wq2_ref, sk_ref,
                        h_ref, th_ref, e1_ref, s2_ref, e2_ref):
    K = PEER_TOPK
    half = PEER_QDIM // 2
    h = _rms(x_ref[...], nw_ref[...], D_MODEL)
    h1 = h.astype(BF16)
    h2 = (h - h1.astype(F32)).astype(BF16)
    h_ref[...] = h1
    nt = (((1,), (1,)), ((), ()))
    wq1 = wq1_ref[...]
    q_t = (lax.dot_general(wq1, h1, nt, preferred_element_type=F32)
           + lax.dot_general(wq1, h2, nt, preferred_element_type=F32)
           + lax.dot_general(wq2_ref[...], h1, nt, preferred_element_type=F32))
    for hd in range(PEER_HEADS):
        scores, tops = [], []
        for p in range(2):
            r0 = (hd * 2 + p) * half
            s = _dot_hi(sk_ref[p], q_t[r0:r0 + half, :])
            scores.append(s)
            tops.append(_topk_vals(s, K))
        sv1, sv2 = tops
        cand = jnp.concatenate([sv1[a:a + 1, :] + sv2 for a in range(K)], axis=0)
        cv = _topk_vals(cand, K)
        tau = cv[K - 1:K, :]
        zsum = jnp.sum(jnp.exp(cv - cv[0:1, :]), axis=0, keepdims=True)
        s1, s2 = scores
        theta = jnp.full(s1.shape, jnp.inf, F32)
        for b in range(K):
            vb = sv2[b:b + 1, :]
            theta = jnp.where(s1 + vb >= tau, vb, theta)
        th_ref[hd] = theta
        e1_ref[hd] = jnp.exp(s1 - sv1[0:1, :])
        s2_ref[hd] = s2
        e2_ref[hd] = jnp.exp(s2 - sv2[0:1, :]) / zsum


def _peer_dense_kernel(h_ref, x_ref, u_ref, vt_ref, th_ref, e1_ref, s2_ref, e2_ref,
                       o_ref, acc_ref, p_ref, *, chunk):
    c = pl.program_id(1)
    nk = N_KEYS

    @pl.when(c == 0)
    def _():
        acc_ref[...] = jnp.zeros_like(acc_ref)

    nt = (((1,), (1,)), ((), ()))
    h_t = lax.dot_general(u_ref[...], h_ref[...], nt, preferred_element_type=F32)
    for il in range(chunk):
        i1 = c * chunk + il
        a = h_t[il * nk:(il + 1) * nk, :]
        act = 0.5 * a * (1.0 + lax.erf(a * (2.0 ** -0.5)))
        w = jnp.zeros(a.shape, F32)
        for hd in range(PEER_HEADS):
            th = th_ref[hd, pl.ds(i1, 1), :]
            e1 = e1_ref[hd, pl.ds(i1, 1), :]
            w = w + jnp.where(s2_ref[hd] >= th, e2_ref[hd], 0.0) * e1
        p_ref[il * nk:(il + 1) * nk, :] = (w * act).astype(BF16)
    acc_ref[...] += jnp.dot(vt_ref[...], p_ref[...], preferred_element_type=F32)

    @pl.when(c == pl.num_programs(1) - 1)
    def _():
        o_ref[...] = x_ref[...] + acc_ref[...].T


def _peer(x1, norm_w, wq_t1, wq_t2, sub_keys, u_bf, vt_bf):
    T = x1.shape[0]
    H, nk = PEER_HEADS, N_KEYS
    tb_s = 256
    row = lambda i: (i, 0)
    fixed2 = lambda i: (0, 0)
    tok3 = lambda i: (0, 0, i)
    head_shape = jax.ShapeDtypeStruct((H, nk, T), F32)
    h_bf, theta, e1, s2, e2 = pl.pallas_call(
        _peer_select_kernel,
        grid=(T // tb_s,),
        in_specs=[pl.BlockSpec((tb_s, D_MODEL), row),
                  pl.BlockSpec((1, D_MODEL), fixed2),
                  pl.BlockSpec((H * PEER_QDIM, D_MODEL), fixed2),
                  pl.BlockSpec((H * PEER_QDIM, D_MODEL), fixed2),
                  pl.BlockSpec((2, nk, PEER_QDIM // 2), lambda i: (0, 0, 0))],
        out_specs=[pl.BlockSpec((tb_s, D_MODEL), row)] + [pl.BlockSpec((H, nk, tb_s), tok3)] * 4,
        out_shape=[jax.ShapeDtypeStruct((T, D_MODEL), BF16)] + [head_shape] * 4,
        compiler_params=_cparams("parallel"),
    )(x1, norm_w.reshape(1, D_MODEL), wq_t1, wq_t2, sub_keys)

    tb = 512
    chunk = 8
    n_exp = nk * nk
    tok = lambda i, c: (i, 0)
    tok3d = lambda i, c: (0, 0, i)
    return pl.pallas_call(
        functools.partial(_peer_dense_kernel, chunk=chunk),
        grid=(T // tb, nk // chunk),
        in_specs=[pl.BlockSpec((tb, D_MODEL), tok),
                  pl.BlockSpec((tb, D_MODEL), tok),
                  pl.BlockSpec((chunk * nk, D_MODEL), lambda i, c: (c, 0)),
                  pl.BlockSpec((D_MODEL, chunk * nk), lambda i, c: (0, c))]
                 + [pl.BlockSpec((H, nk, tb), tok3d)] * 4,
        out_specs=pl.BlockSpec((tb, D_MODEL), tok),
        out_shape=jax.ShapeDtypeStruct((T, D_MODEL), F32),
        scratch_shapes=[pltpu.VMEM((D_MODEL, tb), F32),
                        pltpu.VMEM((chunk * nk, tb), BF16)],
        compiler_params=_cparams("parallel", "arbitrary"),
    )(h_bf, x1, u_bf, vt_bf, theta, e1, s2, e2)


def kernel(x, norm1_w, w_in, conv_w, conv_b, dt_bias, a_log, d_skip, ssd_norm_w, sb_out_w,
           nsa_q_norm_w, nsa_k_norm_w, nsa_cmp_pos, nsa_cmp_w, nsa_out_w, w_out, norm2_w,
           peer_w_q, peer_sub_keys, peer_u, peer_v):
    B, S, _ = x.shape
    depth = w_in.shape[0]
    assert S == 2048, "NSA tiling is laid out for 2048-token sequences"
    T = B * S
    ropes = _rope_tables(S)
    xt = x.reshape(T, D_MODEL)
    for l in range(depth):
        w_in_pad = _pad_cols(w_in[l], _IN_COLMAP).astype(BF16)
        z, xbc, dt, sbq, sbk, sbv, nq, nkv, gate = _inproj(xt, norm1_w[l], w_in_pad)
        y_ssd = _ssd(z, xbc, dt, conv_w[l], conv_b[l], dt_bias[l], a_log[l], d_skip[l], ssd_norm_w[l], B, S)
        y_sb = _sb(sbq, sbk, sbv, B, S)
        q_r, kc, vc, ksw = _nsa_prep(nq, nkv, ropes, nsa_q_norm_w[l], nsa_k_norm_w[l], B, S)
        k_cmp, v_cmp = _nsa_cmp(kc, vc, nsa_cmp_pos[l], nsa_cmp_w[l], nsa_k_norm_w[l, 0], B, S)
        y_nsa = _nsa_attn(q_r, k_cmp, v_cmp, ksw, gate, nsa_out_w[l], B, S)
        w_out_pad = _pad_cols(w_out[l].T, _OUT_ROWMAP).T.astype(BF16)
        x1 = _outproj(xt, y_ssd, y_sb, y_nsa, sb_out_w[l], w_out_pad)
        wq_t = peer_w_q[l].T
        wq_t1 = wq_t.astype(BF16)
        wq_t2 = (wq_t - wq_t1.astype(F32)).astype(BF16)
        xt = _peer(x1, norm2_w[l], wq_t1, wq_t2, peer_sub_keys[l],
                   peer_u[l].astype(BF16), peer_v[l].T.astype(BF16))
    return xt.reshape(B, S, D_MODEL)
```

```python
import functools
import math

import numpy as np
import jax
import jax.numpy as jnp
from jax import lax
from jax.experimental import pallas as pl
from jax.experimental.pallas import tpu as pltpu

F32 = jnp.float32
BF16 = jnp.bfloat16

D_MODEL = 1024
HEAD_DIM = 64
SSD_WIDTH = 512
SSD_HEADS = 8
SSD_GROUPS = 2
SSD_STATE = 128
SSD_CONV = 4
SSD_CHUNK = 128
SSD_CONV_DIM = SSD_WIDTH + 2 * SSD_GROUPS * SSD_STATE
SB_WIDTH = 256
SB_HEADS = 4
NSA_WIDTH = 256
NSA_HEADS = 4
CMP_LEN = 32
CMP_STRIDE = 16
SEL_BLOCK = 64
SEL_TOPK = 16
WINDOW = 512
FORCE_SCORE = 1e9
ROT_DIM = 16
ROPE_THETA = 500000.0
PEER_HEADS = 8
N_KEYS = 128
PEER_TOPK = 16
PEER_QDIM = 256
NORM_EPS = 1e-6
NEG_BIG = -1e30

LANE = 128
SUBLANE = 8
V7X_VMEM_BYTES = 64 * 1024 * 1024
VMEM_LIMIT = V7X_VMEM_BYTES - 12 * 1024 * 1024

SLOT = LANE
Q_TILE = 128


def _cparams(*sem):
    return pltpu.CompilerParams(dimension_semantics=sem, vmem_limit_bytes=VMEM_LIMIT)


def _iota(shape, dim):
    return lax.broadcasted_iota(jnp.int32, shape, dim)


def _dot(a, b):
    return jnp.dot(a.astype(BF16), b.astype(BF16), preferred_element_type=F32)


def _dot_nt(a, b):
    return lax.dot_general(a.astype(BF16), b.astype(BF16), (((1,), (1,)), ((), ())),
                           preferred_element_type=F32)


def _split(a, n):
    parts, r = [], a
    for i in range(n):
        p = r.astype(BF16)
        parts.append(p)
        if i + 1 < n:
            r = r - p.astype(F32)
    return parts


def _dot_split_lhs(a, b_exact, n=3):
    acc = None
    for p in _split(a, n):
        t = jnp.dot(p, b_exact, preferred_element_type=F32)
        acc = t if acc is None else acc + t
    return acc


def _dot_split_rhs(a_exact, b, n=3):
    acc = None
    for p in _split(b, n):
        t = jnp.dot(a_exact, p, preferred_element_type=F32)
        acc = t if acc is None else acc + t
    return acc


def _dot_hi(a, b):
    a1, a2 = _split(a, 2)
    b1, b2 = _split(b, 2)
    return (jnp.dot(a1, b1, preferred_element_type=F32) + jnp.dot(a1, b2, preferred_element_type=F32)
            + jnp.dot(a2, b1, preferred_element_type=F32))


def _sigmoid(x):
    return 1.0 / (1.0 + jnp.exp(-x))


def _softplus(x):
    return jnp.maximum(x, 0.0) + jnp.log1p(jnp.exp(-jnp.abs(x)))


def _rms(x, w, n):
    ms = jnp.sum(x * x, axis=-1, keepdims=True) * (1.0 / n)
    return x * lax.rsqrt(ms + NORM_EPS) * w


def _softmax_rows(s):
    m = jnp.max(s, axis=-1, keepdims=True)
    e = jnp.exp(s - m)
    return e / jnp.sum(e, axis=-1, keepdims=True)


_IN_SEGS = (("z", 512, F32), ("xbc", 1024, F32), ("dt", 128, F32),
            ("sbq", 512, BF16), ("sbk", 512, BF16), ("sbv", 512, BF16),
            ("nq", 512, F32), ("nkv", 768, F32), ("gate", 128, F32))
_IN_TOTAL = sum(s[1] for s in _IN_SEGS)


def _inproj_colmap():
    src = []

    def real(a, n, pad_to=None):
        src.extend(range(a, a + n))
        if pad_to:
            src.extend([-1] * (pad_to - n))

    def heads(a, nheads):
        for h in range(nheads):
            src.extend(range(a + h * HEAD_DIM, a + (h + 1) * HEAD_DIM))
            src.extend([-1] * (SLOT - HEAD_DIM))

    o_xbc = SSD_WIDTH
    o_dt = o_xbc + SSD_CONV_DIM
    o_sb = o_dt + SSD_HEADS
    o_nq = o_sb + 3 * SB_WIDTH
    o_nkv = o_nq + NSA_WIDTH
    o_gate = o_nkv + 6 * HEAD_DIM
    real(0, SSD_WIDTH)
    real(o_xbc, SSD_CONV_DIM)
    real(o_dt, SSD_HEADS, LANE)
    heads(o_sb, SB_HEADS)
    heads(o_sb + SB_WIDTH, SB_HEADS)
    heads(o_sb + 2 * SB_WIDTH, SB_HEADS)
    heads(o_nq, NSA_HEADS)
    heads(o_nkv, 6)
    real(o_gate, 3 * NSA_HEADS, LANE)
    src = np.asarray(src, np.int32)
    assert src.shape[0] == _IN_TOTAL
    return src


_IN_COLMAP = _inproj_colmap()


def _pad_cols(w, colmap):
    g = jnp.take(w, jnp.asarray(np.maximum(colmap, 0)), axis=-1)
    return jnp.where(jnp.asarray(colmap >= 0), g, 0.0)


def _pad_heads_vec(v, nheads):
    v = v.reshape(nheads, HEAD_DIM)
    v = jnp.pad(v, ((0, 0), (0, SLOT - HEAD_DIM)))
    return v.reshape(1, nheads * SLOT)


def _inproj_kernel(x_ref, nw_ref, w_ref, *out_refs):
    x = x_ref[...]
    h = _rms(x, nw_ref[...], D_MODEL).astype(BF16)
    off = 0
    for (name, width, dt), o_ref in zip(_IN_SEGS, out_refs):
        o_ref[...] = jnp.dot(h, w_ref[:, off:off + width], preferred_element_type=F32).astype(dt)
        off += width


def _inproj(xt, norm_w, w_pad):
    T = xt.shape[0]
    tm = 256
    out_shape = [jax.ShapeDtypeStruct((T, w), dt) for (_, w, dt) in _IN_SEGS]
    out_specs = [pl.BlockSpec((tm, w), lambda i: (i, 0)) for (_, w, _) in _IN_SEGS]
    return pl.pallas_call(
        _inproj_kernel,
        grid=(T // tm,),
        in_specs=[pl.BlockSpec((tm, D_MODEL), lambda i: (i, 0)),
                  pl.BlockSpec((1, D_MODEL), lambda i: (0, 0)),
                  pl.BlockSpec((D_MODEL, _IN_TOTAL), lambda i: (0, 0))],
        out_specs=out_specs,
        out_shape=out_shape,
        compiler_params=_cparams("parallel"),
    )(xt, norm_w.reshape(1, D_MODEL), w_pad)


def _ssd_kernel(z_ref, xbc_ref, dt_ref, cw_ref, cb_ref, dtb_ref, alog_ref, dsk_ref, nw_ref,
                o_ref, xbuf, state):
    L = SSD_CHUNK
    c = pl.program_id(1)

    @pl.when(c == 0)
    def _():
        xbuf[0:SUBLANE, :] = jnp.zeros((SUBLANE, SSD_CONV_DIM), F32)
        state[...] = jnp.zeros_like(state)

    xcur = xbc_ref[...]
    xbuf[SUBLANE:SUBLANE + L, :] = xcur
    acc = jnp.broadcast_to(cb_ref[...], (L, SSD_CONV_DIM))
    for k in range(SSD_CONV):
        s0 = SUBLANE - (SSD_CONV - 1) + k
        acc = acc + xbuf[s0:s0 + L, :] * cw_ref[k:k + 1, :]
    xbuf[0:SUBLANE, :] = xcur[L - SUBLANE:L, :]
    xc = acc * _sigmoid(acc)

    xs = xc[:, 0:SSD_WIDTH]
    dt = _softplus(dt_ref[...] + dtb_ref[...])
    a_dec = dt * (-jnp.exp(alog_ref[...]))

    row = _iota((L, L), 0)
    col = _iota((L, L), 1)
    tril = row >= col
    tri = tril.astype(BF16)
    cs = _dot_split_rhs(tri, a_dec)
    e64 = (_iota((LANE, SSD_WIDTH), 0) == _iota((LANE, SSD_WIDTH), 1) // HEAD_DIM).astype(BF16)
    e128 = (_iota((LANE, SSD_HEADS * LANE), 0) == _iota((LANE, SSD_HEADS * LANE), 1) // LANE).astype(BF16)
    dt_e = _dot_split_lhs(dt, e64)
    cs_e = _dot_split_lhs(cs, e64)
    cs_e2 = _dot_split_lhs(cs, e128)
    cs_t = cs.T
    cs_last = cs_e[L - 1:L, :]
    x_dt = xs * dt_e
    x_dec = x_dt * jnp.exp(cs_last - cs_e)
    in_dec = jnp.exp(cs_e)
    chunk_dec = jnp.exp(cs_last)
    lane_lo = _iota((L, LANE), 1) < HEAD_DIM

    ys = []
    for g in range(SSD_GROUPS):
        b_g = xc[:, SSD_WIDTH + g * SSD_STATE:SSD_WIDTH + (g + 1) * SSD_STATE]
        c_g = xc[:, SSD_WIDTH + (SSD_GROUPS + g) * SSD_STATE:SSD_WIDTH + (SSD_GROUPS + g + 1) * SSD_STATE]
        cb = _dot_nt(c_g, b_g)
        b_gt = b_g.T
        for m in range(2 * g, 2 * g + 2):
            lanes = slice(m * LANE, (m + 1) * LANE)
            xp = x_dt[:, lanes]
            yd = []
            for h in (2 * m, 2 * m + 1):
                diff = cs_e2[:, h * LANE:(h + 1) * LANE] - cs_t[h:h + 1, :]
                lm = jnp.exp(jnp.where(tril, diff, -jnp.inf))
                yd.append(_dot(lm * cb, xp))
            y_diag = jnp.where(lane_lo, yd[0], yd[1])
            s_prev = state[m]
            y_off = _dot(c_g, s_prev) * in_dec[:, lanes]
            state[m] = s_prev * chunk_dec[:, lanes] + _dot(b_gt, x_dec[:, lanes])
            ys.append(y_diag + y_off + xs[:, lanes] * dsk_ref[:, lanes])
    y = jnp.concatenate(ys, axis=1)
    zt = z_ref[...]
    o_ref[...] = _rms(y * (zt * _sigmoid(zt)), nw_ref[...], SSD_WIDTH)


def _ssd(z, xbc, dt, conv_w, conv_b, dt_bias, a_log, d_skip, norm_w, B, S):
    T = B * S
    L = SSD_CHUNK
    nc = S // L
    pad8 = lambda v: jnp.pad(v, (0, LANE - SSD_HEADS)).reshape(1, LANE)
    row = lambda b, c: (b * nc + c, 0)
    fixed = lambda b, c: (0, 0)
    return pl.pallas_call(
        _ssd_kernel,
        grid=(B, nc),
        in_specs=[pl.BlockSpec((L, SSD_WIDTH), row),
                  pl.BlockSpec((L, SSD_CONV_DIM), row),
                  pl.BlockSpec((L, LANE), row),
                  pl.BlockSpec((SSD_CONV, SSD_CONV_DIM), fixed),
                  pl.BlockSpec((1, SSD_CONV_DIM), fixed),
                  pl.BlockSpec((1, LANE), fixed),
                  pl.BlockSpec((1, LANE), fixed),
                  pl.BlockSpec((1, SSD_WIDTH), fixed),
                  pl.BlockSpec((1, SSD_WIDTH), fixed)],
        out_specs=pl.BlockSpec((L, SSD_WIDTH), row),
        out_shape=jax.ShapeDtypeStruct((T, SSD_WIDTH), F32),
        scratch_shapes=[pltpu.VMEM((SUBLANE + L, SSD_CONV_DIM), F32),
                        pltpu.VMEM((SSD_HEADS // 2, SSD_STATE, LANE), F32)],
        compiler_params=_cparams("parallel", "arbitrary"),
    )(z, xbc, dt, conv_w, conv_b.reshape(1, -1), pad8(dt_bias), pad8(a_log),
      jnp.repeat(d_skip, HEAD_DIM).reshape(1, SSD_WIDTH), norm_w.reshape(1, SSD_WIDTH))


SB_LOG_CUTOFF = -104.0


def _sb_kernel(q_ref, k_ref, v_ref, o_ref, cf_ref):
    tq = Q_TILE
    H = SB_HEADS
    qi = pl.program_id(1)
    row = _iota((tq, tq), 0)
    col = _iota((tq, tq), 1)
    strict = col < row
    r2 = _iota((tq, 2 * tq), 0)
    c2 = _iota((tq, 2 * tq), 1)
    umat = ((r2 > c2) | (c2 >= tq)).astype(BF16)
    scale = HEAD_DIM ** -0.5
    o_ref[...] = jnp.zeros_like(o_ref)
    cf_ref[...] = jnp.zeros_like(cf_ref)

    def cond(c):
        i, live = c
        return jnp.logical_and(i <= qi, live > 0)

    def body(c):
        i, _ = c
        ks = pl.multiple_of((qi - i) * tq, tq)
        mask = jnp.logical_or(i > 0, strict)
        cf_max = None
        for h in range(H):
            lanes = slice(h * SLOT, (h + 1) * SLOT)
            kt = k_ref[pl.ds(ks, tq), lanes]
            vt = v_ref[pl.ds(ks, tq), lanes]
            z = _dot_nt(q_ref[:, lanes], kt) * scale
            l1p = jnp.log1p(jnp.exp(-jnp.abs(z)))
            log_beta = jnp.where(mask, jnp.minimum(z, 0.0) - l1p, -jnp.inf)
            log_stay = jnp.where(mask, -jnp.maximum(z, 0.0) - l1p, 0.0)
            r = _dot_split_lhs(log_stay, umat, 2)
            cf = cf_ref[h]
            w = jnp.exp(log_beta + r[:, 0:tq] + cf)
            o_ref[:, lanes] += _dot(w, vt)
            cf = cf + r[:, tq:2 * tq]
            cf_ref[h] = cf
            cf_max = cf if cf_max is None else jnp.maximum(cf_max, cf)
        live = (jnp.max(cf_max) > SB_LOG_CUTOFF).astype(jnp.int32)
        return i + 1, live

    lax.while_loop(cond, body, (jnp.int32(0), jnp.int32(1)))


def _sb(sbq, sbk, sbv, B, S):
    T = B * S
    nq = S // Q_TILE
    W = SB_HEADS * SLOT
    return pl.pallas_call(
        _sb_kernel,
        grid=(B, nq),
        in_specs=[pl.BlockSpec((Q_TILE, W), lambda b, i: (b * nq + i, 0)),
                  pl.BlockSpec((S, W), lambda b, i: (b, 0)),
                  pl.BlockSpec((S, W), lambda b, i: (b, 0))],
        out_specs=pl.BlockSpec((Q_TILE, W), lambda b, i: (b * nq + i, 0)),
        out_shape=jax.ShapeDtypeStruct((T, W), F32),
        scratch_shapes=[pltpu.VMEM((SB_HEADS, Q_TILE, SLOT), F32)],
        compiler_params=_cparams("parallel", "arbitrary"),
    )(sbq, sbk, sbv)


def _rope_tables(S):
    pos = jnp.arange(S, dtype=F32)
    inv_freq = ROPE_THETA ** (-jnp.arange(0, ROT_DIM, 2, dtype=F32) / ROT_DIM)
    ang = pos[:, None] * inv_freq[None, :]
    cos, sin = jnp.cos(ang), jnp.sin(ang)
    half = ROT_DIM // 2
    ones = jnp.ones((S, SLOT - ROT_DIM), F32)
    zeros_h = jnp.zeros((S, half), F32)
    zeros_r = jnp.zeros((S, SLOT - ROT_DIM), F32)
    cosf = jnp.concatenate([cos, cos, ones], axis=1)
    sin_a = jnp.concatenate([-sin, zeros_h, zeros_r], axis=1)
    sin_b = jnp.concatenate([zeros_h, sin, zeros_r], axis=1)
    return cosf, sin_a, sin_b


def _nsa_prep_kernel(q_ref, kv_ref, cos_ref, sa_ref, sb_ref, qw_ref, kw_ref,
                     qo_ref, kc_ref, vc_ref, ksw_ref):
    cos, sa, sb = cos_ref[...], sa_ref[...], sb_ref[...]
    half = ROT_DIM // 2

    def rope(x):
        return x * cos + pltpu.roll(x, SLOT - half, 1) * sa + pltpu.roll(x, half, 1) * sb

    qw = qw_ref[...]
    for h in range(NSA_HEADS):
        lanes = slice(h * SLOT, (h + 1) * SLOT)
        qo_ref[:, lanes] = rope(_rms(q_ref[:, lanes], qw, HEAD_DIM)).astype(BF16)
    kc_ref[...] = rope(kv_ref[:, 0:SLOT])
    vc_ref[...] = kv_ref[:, SLOT:2 * SLOT]
    ksw_ref[:, 0:SLOT] = rope(_rms(kv_ref[:, 2 * SLOT:3 * SLOT], kw_ref[1:2, :], HEAD_DIM)).astype(BF16)
    ksw_ref[:, SLOT:2 * SLOT] = kv_ref[:, 3 * SLOT:4 * SLOT].astype(BF16)
    ksw_ref[:, 2 * SLOT:3 * SLOT] = rope(_rms(kv_ref[:, 4 * SLOT:5 * SLOT], kw_ref[2:3, :], HEAD_DIM)).astype(BF16)
    ksw_ref[:, 3 * SLOT:4 * SLOT] = kv_ref[:, 5 * SLOT:6 * SLOT].astype(BF16)


def _nsa_prep(nq, nkv, ropes, q_norm_w, k_norm_w, B, S):
    T = B * S
    ts = 256
    nt = S // ts
    cosf, sin_a, sin_b = ropes
    pad = lambda v: jnp.pad(v, ((0, 0), (0, SLOT - HEAD_DIM)))
    row = lambda b, i: (b * nt + i, 0)
    pos = lambda b, i: (i, 0)
    fixed = lambda b, i: (0, 0)
    return pl.pallas_call(
        _nsa_prep_kernel,
        grid=(B, nt),
        in_specs=[pl.BlockSpec((ts, NSA_HEADS * SLOT), row),
                  pl.BlockSpec((ts, 6 * SLOT), row),
                  pl.BlockSpec((ts, SLOT), pos),
                  pl.BlockSpec((ts, SLOT), pos),
                  pl.BlockSpec((ts, SLOT), pos),
                  pl.BlockSpec((1, SLOT), fixed),
                  pl.BlockSpec((3, SLOT), fixed)],
        out_specs=[pl.BlockSpec((ts, NSA_HEADS * SLOT), row),
                   pl.BlockSpec((ts, SLOT), row),
                   pl.BlockSpec((ts, SLOT), row),
                   pl.BlockSpec((ts, 4 * SLOT), row)],
        out_shape=[jax.ShapeDtypeStruct((T, NSA_HEADS * SLOT), BF16),
                   jax.ShapeDtypeStruct((T, SLOT), F32),
                   jax.ShapeDtypeStruct((T, SLOT), F32),
                   jax.ShapeDtypeStruct((T, 4 * SLOT), BF16)],
        compiler_params=_cparams("parallel", "parallel"),
    )(nq, nkv, cosf, sin_a, sin_b, pad(q_norm_w.reshape(1, HEAD_DIM)), pad(k_norm_w))


def _nsa_cmp_kernel(kc_ref, vc_ref, pos_ref, w_ref, knw_ref, ko_ref, vo_ref):
    nseg = ko_ref.shape[0]
    half = CMP_LEN // 2
    for which, (src, dst) in enumerate(((kc_ref, ko_ref), (vc_ref, vo_ref))):
        f1 = jnp.zeros((nseg, SLOT), F32)
        f2 = jnp.zeros((nseg, SLOT), F32)
        for p in range(half):
            rows = src[pl.ds(p, nseg, stride=CMP_STRIDE), :]
            f1 = f1 + _dot(rows + pos_ref[which, p:p + 1, :], w_ref[which, p])
            f2 = f2 + _dot(rows + pos_ref[which, half + p:half + p + 1, :], w_ref[which, half + p])
        pre = f1 + pltpu.roll(f2, nseg - 1, 0)
        if which == 0:
            pre = _rms(pre, knw_ref[...], HEAD_DIM)
        dst[...] = pre


def _nsa_cmp(kc, vc, cmp_pos, cmp_w, k_norm_w0, B, S):
    nseg = S // CMP_STRIDE
    pos = jnp.pad(cmp_pos, ((0, 0), (0, 0), (0, SLOT - HEAD_DIM)))
    w = cmp_w.reshape(2, CMP_LEN, HEAD_DIM, HEAD_DIM)
    w = jnp.pad(w, ((0, 0), (0, 0), (0, SLOT - HEAD_DIM), (0, SLOT - HEAD_DIM))).astype(BF16)
    knw = jnp.pad(k_norm_w0.reshape(1, HEAD_DIM), ((0, 0), (0, SLOT - HEAD_DIM)))
    per_b = lambda b: (b, 0)
    return pl.pallas_call(
        _nsa_cmp_kernel,
        grid=(B,),
        in_specs=[pl.BlockSpec((S, SLOT), per_b),
                  pl.BlockSpec((S, SLOT), per_b),
                  pl.BlockSpec((2, CMP_LEN, SLOT), lambda b: (0, 0, 0)),
                  pl.BlockSpec((2, CMP_LEN, SLOT, SLOT), lambda b: (0, 0, 0, 0)),
                  pl.BlockSpec((1, SLOT), lambda b: (0, 0))],
        out_specs=[pl.BlockSpec((nseg, SLOT), per_b), pl.BlockSpec((nseg, SLOT), per_b)],
        out_shape=[jax.ShapeDtypeStruct((B * nseg, SLOT), F32),
                   jax.ShapeDtypeStruct((B * nseg, SLOT), F32)],
        compiler_params=_cparams("parallel"),
    )(kc, vc, pos, w, knw)


def _nsa_attn_kernel(q_ref, kc_ref, vc_ref, ksw_ref, g_ref, ow_ref, o_ref, *, S):
    tq = Q_TILE
    n_cmp_pad = S // CMP_STRIDE
    n_sel = S // SEL_BLOCK
    n_top = min(SEL_TOPK, n_sel)
    n_win = WINDOW + tq
    scale = HEAD_DIM ** -0.5
    qi = pl.program_id(1)
    t0 = qi * tq

    t_c = t0 + _iota((tq, n_cmp_pad), 0)
    c_c = _iota((tq, n_cmp_pad), 1)
    vis = (c_c * CMP_STRIDE + (CMP_LEN - 1)) <= t_c
    kc = kc_ref[...]
    vc = vc_ref[...]
    qs = [q_ref[:, h * SLOT:(h + 1) * SLOT] for h in range(NSA_HEADS)]
    o_cmp = []
    p_sum = jnp.zeros((tq, n_cmp_pad), F32)
    for h in range(NSA_HEADS):
        s = jnp.where(vis, _dot_nt(qs[h], kc) * scale, NEG_BIG)
        p = jnp.where(vis, _softmax_rows(s), 0.0)
        o_cmp.append(_dot(p, vc))
        p_sum = p_sum + p

    r_o = _iota((n_cmp_pad, LANE), 0) * CMP_STRIDE
    j_o = _iota((n_cmp_pad, LANE), 1) * SEL_BLOCK
    overlap = ((r_o < j_o + SEL_BLOCK) & (r_o + CMP_LEN > j_o)
               & (_iota((n_cmp_pad, LANE), 0) < n_cmp_pad - 1)).astype(BF16)
    imp_t = _dot_split_lhs(p_sum, overlap).T
    j_t = _iota((LANE, tq), 0)
    t_t = t0 + _iota((LANE, tq), 1)
    blk = t_t // SEL_BLOCK
    forced = (j_t == 0) | (j_t == blk) | (j_t == blk - 1)
    causal = (j_t * SEL_BLOCK <= t_t) & (j_t < n_sel)
    val = jnp.where(causal, jnp.where(forced, FORCE_SCORE, imp_t), -jnp.inf)
    rank = jnp.zeros((LANE, tq), jnp.int32)
    for jp in range(n_sel):
        vj = val[jp:jp + 1, :]
        beats = (vj > val) | ((vj == val) & (j_t > jp))
        rank = rank + beats.astype(jnp.int32)
    sel_t = (causal & (rank < n_top)).astype(F32)
    sel = sel_t.T.astype(BF16)
    expand = (_iota((LANE, S), 0) == _iota((LANE, S), 1) // SEL_BLOCK).astype(BF16)
    sel_keys = jnp.dot(sel, expand, preferred_element_type=F32)

    t_s = t0 + _iota((tq, S), 0)
    k_s = _iota((tq, S), 1)
    ok_sel = (sel_keys > 0.5) & (k_s <= t_s)
    k_sel = ksw_ref[:, 0:SLOT]
    v_sel = ksw_ref[:, SLOT:2 * SLOT]
    o_sel = []
    for h in range(NSA_HEADS):
        s = jnp.where(ok_sel, _dot_nt(qs[h], k_sel) * scale, NEG_BIG)
        o_sel.append(_dot(_softmax_rows(s), v_sel))

    start = pl.multiple_of(jnp.clip(t0 - WINDOW, 0, S - n_win), tq)
    k_win = ksw_ref[pl.ds(start, n_win), 2 * SLOT:3 * SLOT]
    v_win = ksw_ref[pl.ds(start, n_win), 3 * SLOT:4 * SLOT]
    rel = (t0 + _iota((tq, n_win), 0)) - (start + _iota((tq, n_win), 1))
    ok_win = (rel >= 0) & (rel < WINDOW)
    o_win = []
    for h in range(NSA_HEADS):
        s = jnp.where(ok_win, _dot_nt(qs[h], k_win) * scale, NEG_BIG)
        o_win.append(_dot(_softmax_rows(s), v_win))

    gate = _sigmoid(g_ref[...])
    n_g = 3 * NSA_HEADS
    e_g = (_iota((LANE, n_g * SLOT), 0) == _iota((LANE, n_g * SLOT), 1) // SLOT).astype(BF16)
    g_e = _dot_split_lhs(gate, e_g)
    outs = []
    for h in range(NSA_HEADS):
        g0 = g_e[:, (3 * h) * SLOT:(3 * h + 1) * SLOT]
        g1 = g_e[:, (3 * h + 1) * SLOT:(3 * h + 2) * SLOT]
        g2 = g_e[:, (3 * h + 2) * SLOT:(3 * h + 3) * SLOT]
        outs.append(g0 * o_cmp[h] + g1 * o_sel[h] + g2 * o_win[h])
    o = jnp.concatenate(outs, axis=1)
    o_ref[...] = _rms(o, ow_ref[...], NSA_WIDTH)


def _nsa_attn(q_r, k_cmp, v_cmp, ksw, gate, out_w, B, S):
    T = B * S
    nq = S // Q_TILE
    nseg = S // CMP_STRIDE
    row = lambda b, i: (b * nq + i, 0)
    per_b = lambda b, i: (b, 0)
    return pl.pallas_call(
        functools.partial(_nsa_attn_kernel, S=S),
        grid=(B, nq),
        in_specs=[pl.BlockSpec((Q_TILE, NSA_HEADS * SLOT), row),
                  pl.BlockSpec((nseg, SLOT), per_b),
                  pl.BlockSpec((nseg, SLOT), per_b),
                  pl.BlockSpec((S, 4 * SLOT), per_b),
                  pl.BlockSpec((Q_TILE, LANE), row),
                  pl.BlockSpec((1, NSA_HEADS * SLOT), lambda b, i: (0, 0))],
        out_specs=pl.BlockSpec((Q_TILE, NSA_HEADS * SLOT), row),
        out_shape=jax.ShapeDtypeStruct((T, NSA_HEADS * SLOT), F32),
        compiler_params=_cparams("parallel", "parallel"),
    )(q_r, k_cmp, v_cmp, ksw, gate, _pad_heads_vec(out_w, NSA_HEADS))


def _outproj_kernel(x_ref, yssd_ref, ysb_ref, ynsa_ref, sbw_ref, w_ref, o_ref):
    ysb = _rms(ysb_ref[...], sbw_ref[...], SB_WIDTH)
    w0 = SSD_WIDTH
    w1 = w0 + SB_HEADS * SLOT
    w2 = w1 + NSA_HEADS * SLOT
    acc = x_ref[...]
    acc = acc + jnp.dot(yssd_ref[...].astype(BF16), w_ref[0:w0, :], preferred_element_type=F32)
    acc = acc + jnp.dot(ysb.astype(BF16), w_ref[w0:w1, :], preferred_element_type=F32)
    acc = acc + jnp.dot(ynsa_ref[...].astype(BF16), w_ref[w1:w2, :], preferred_element_type=F32)
    o_ref[...] = acc


def _outproj_rowmap():
    src = list(range(SSD_WIDTH))
    for base, nheads in ((SSD_WIDTH, SB_HEADS), (SSD_WIDTH + SB_WIDTH, NSA_HEADS)):
        for h in range(nheads):
            src.extend(range(base + h * HEAD_DIM, base + (h + 1) * HEAD_DIM))
            src.extend([-1] * (SLOT - HEAD_DIM))
    return np.asarray(src, np.int32)


_OUT_ROWMAP = _outproj_rowmap()


def _outproj(xt, y_ssd, y_sb, y_nsa, sb_out_w, w_pad):
    T = xt.shape[0]
    tm = 256
    kdim = _OUT_ROWMAP.shape[0]
    row = lambda i: (i, 0)
    fixed = lambda i: (0, 0)
    return pl.pallas_call(
        _outproj_kernel,
        grid=(T // tm,),
        in_specs=[pl.BlockSpec((tm, D_MODEL), row),
                  pl.BlockSpec((tm, SSD_WIDTH), row),
                  pl.BlockSpec((tm, SB_HEADS * SLOT), row),
                  pl.BlockSpec((tm, NSA_HEADS * SLOT), row),
                  pl.BlockSpec((1, SB_HEADS * SLOT), fixed),
                  pl.BlockSpec((kdim, D_MODEL), fixed)],
        out_specs=pl.BlockSpec((tm, D_MODEL), row),
        out_shape=jax.ShapeDtypeStruct((T, D_MODEL), F32),
        compiler_params=_cparams("parallel"),
    )(xt, y_ssd, y_sb, y_nsa, _pad_heads_vec(sb_out_w, SB_HEADS), w_pad)


def _cmp_exchange(x, y):
    if y is None:
        return x, None
    if x is None:
        return y, None
    return jnp.maximum(x, y), jnp.minimum(x, y)


def _bitonic_sort_desc(u):
    u = list(u)
    n = len(u)
    k = 2
    while k <= n:
        j = k // 2
        while j >= 1:
            for i in range(n):
                l = i ^ j
                if l > i:
                    hi, lo = _cmp_exchange(u[i], u[l])
                    u[i], u[l] = (hi, lo) if (i & k) == 0 else (lo, hi)
            j //= 2
        k *= 2
    return u


def _merge_top(a, b):
    n = len(a)
    c = [_cmp_exchange(a[i], b[n - 1 - i])[0] for i in range(n)]
    d = n // 2
    while d >= 1:
        for i in range(n):
            if (i & d) == 0:
                c[i], c[i + d] = _cmp_exchange(c[i], c[i + d])
        d //= 2
    return c


def _top16_replicated(s):
    groups = s.shape[0] // SUBLANE
    u = _bitonic_sort_desc([s[g * SUBLANE:(g + 1) * SUBLANE, :] for g in range(groups)])
    shift = SUBLANE // 2
    while shift >= 1:
        u = _merge_top(u, [pltpu.roll(x, shift, 0) for x in u])
        shift //= 2
    return u


def _peer_select_kernel(x_ref, nw_ref, wq1_ref, wq2_ref, sk_ref,
                        h_ref, r_ref, e1_ref, rk_ref, e2_ref):
    K = PEER_TOPK
    half = PEER_QDIM // 2
    nk = N_KEYS
    h = _rms(x_ref[...], nw_ref[...], D_MODEL)
    h1 = h.astype(BF16)
    h2 = (h - h1.astype(F32)).astype(BF16)
    h_ref[...] = h1
    nt = (((1,), (1,)), ((), ()))
    wq1 = wq1_ref[...]
    q_t = (lax.dot_general(wq1, h1, nt, preferred_element_type=F32)
           + lax.dot_general(wq1, h2, nt, preferred_element_type=F32)
           + lax.dot_general(wq2_ref[...], h1, nt, preferred_element_type=F32))
    groups = nk // SUBLANE
    for hd in range(PEER_HEADS):
        r0 = hd * 2 * half
        s1 = _dot_hi(sk_ref[0], q_t[r0:r0 + half, :])
        s2 = _dot_hi(sk_ref[1], q_t[r0 + half:r0 + 2 * half, :])
        sv1 = _top16_replicated(s1)
        sv2 = _top16_replicated(s2)
        lists = []
        for a in range(K):
            row = [sv1[a] + sv2[b] for b in range(K // (a + 1))]
            lists.append(row + [None] * (K - len(row)))
        while len(lists) > 1:
            nxt = [_merge_top(lists[i], lists[i + 1]) for i in range(0, len(lists) - 1, 2)]
            if len(lists) % 2:
                nxt.append(lists[-1])
            lists = nxt
        cv = lists[0]
        tau = cv[K - 1]
        zsum = None
        for k in range(K):
            e = jnp.exp(cv[k] - cv[0])
            zsum = e if zsum is None else zsum + e
        inv_z = 1.0 / zsum
        r_parts, rk_parts, e1_parts, e2_parts = [], [], [], []
        for g in range(groups):
            rows = slice(g * SUBLANE, (g + 1) * SUBLANE)
            a1, a2 = s1[rows, :], s2[rows, :]
            cnt = jnp.full(a1.shape, -1.0, F32)
            rk = jnp.zeros(a2.shape, F32)
            for b in range(K):
                cnt = cnt + jnp.where(a1 + sv2[b] >= tau, 1.0, 0.0)
                rk = rk + jnp.where(sv2[b] > a2, 1.0, 0.0)
            r_parts.append(cnt)
            rk_parts.append(rk)
            e1_parts.append(jnp.exp(a1 - sv1[0]))
            e2_parts.append(jnp.exp(a2 - sv2[0]) * inv_z)
        r_ref[hd] = jnp.concatenate(r_parts, axis=0)
        e1_ref[hd] = jnp.concatenate(e1_parts, axis=0)
        rk_ref[hd] = jnp.concatenate(rk_parts, axis=0).astype(BF16)
        e2_ref[hd] = jnp.concatenate(e2_parts, axis=0).astype(BF16)


BF16_ROWS = 2 * SUBLANE


def _peer_dense_kernel(h_ref, x_ref, u_ref, vt_ref, r_ref, e1_ref, rk_ref, e2_ref,
                       o_ref, acc_ref, p_ref):
    c = pl.program_id(1)
    nk = N_KEYS
    tb = h_ref.shape[0]

    @pl.when(c == 0)
    def _():
        acc_ref[...] = jnp.zeros_like(acc_ref)

    nt = (((1,), (1,)), ((), ()))
    h_t = lax.dot_general(u_ref[...], h_ref[...], nt, preferred_element_type=F32)
    row0 = pl.multiple_of(c * SUBLANE, SUBLANE)
    r_rows = [r_ref[hd, pl.ds(row0, SUBLANE), :] for hd in range(PEER_HEADS)]
    e1_rows = [e1_ref[hd, pl.ds(row0, SUBLANE), :] for hd in range(PEER_HEADS)]
    for il in range(SUBLANE):
        a = h_t[il * nk:(il + 1) * nk, :]
        act = (0.5 * a * (1.0 + lax.erf(a * (2.0 ** -0.5)))).astype(BF16)
        rb = [jnp.broadcast_to(r_rows[hd][il:il + 1, :], (BF16_ROWS, tb)).astype(BF16) for hd in range(PEER_HEADS)]
        eb = [jnp.broadcast_to(e1_rows[hd][il:il + 1, :], (BF16_ROWS, tb)).astype(BF16) for hd in range(PEER_HEADS)]
        for j in range(nk // BF16_ROWS):
            rows = pl.ds(j * BF16_ROWS, BF16_ROWS)
            w = None
            for hd in range(PEER_HEADS):
                t = jnp.where(rk_ref[hd, rows, :] <= rb[hd], e2_ref[hd, rows, :], jnp.zeros((), BF16)) * eb[hd]
                w = t if w is None else w + t
            p_ref[pl.ds(il * nk + j * BF16_ROWS, BF16_ROWS), :] = w * act[j * BF16_ROWS:(j + 1) * BF16_ROWS, :]
    acc_ref[...] += jnp.dot(vt_ref[...], p_ref[...], preferred_element_type=F32)

    @pl.when(c == pl.num_programs(1) - 1)
    def _():
        o_ref[...] = x_ref[...] + acc_ref[...].T


def _peer(x1, norm_w, wq_t1, wq_t2, sub_keys, u_bf, vt_bf):
    T = x1.shape[0]
    H, nk = PEER_HEADS, N_KEYS
    tb_s = 256
    row = lambda i: (i, 0)
    fixed2 = lambda i: (0, 0)
    tok3 = lambda i: (0, 0, i)
    head_f32 = jax.ShapeDtypeStruct((H, nk, T), F32)
    head_bf16 = jax.ShapeDtypeStruct((H, nk, T), BF16)
    h_bf, r_cnt, e1, rk2, e2 = pl.pallas_call(
        _peer_select_kernel,
        grid=(T // tb_s,),
        in_specs=[pl.BlockSpec((tb_s, D_MODEL), row),
                  pl.BlockSpec((1, D_MODEL), fixed2),
                  pl.BlockSpec((H * PEER_QDIM, D_MODEL), fixed2),
                  pl.BlockSpec((H * PEER_QDIM, D_MODEL), fixed2),
                  pl.BlockSpec((2, nk, PEER_QDIM // 2), lambda i: (0, 0, 0))],
        out_specs=[pl.BlockSpec((tb_s, D_MODEL), row)] + [pl.BlockSpec((H, nk, tb_s), tok3)] * 4,
        out_shape=[jax.ShapeDtypeStruct((T, D_MODEL), BF16), head_f32, head_f32, head_bf16, head_bf16],
        compiler_params=_cparams("parallel"),
    )(x1, norm_w.reshape(1, D_MODEL), wq_t1, wq_t2, sub_keys)

    tb = 512
    chunk = SUBLANE
    tok = lambda i, c: (i, 0)
    tok3d = lambda i, c: (0, 0, i)
    return pl.pallas_call(
        _peer_dense_kernel,
        grid=(T // tb, nk // chunk),
        in_specs=[pl.BlockSpec((tb, D_MODEL), tok),
                  pl.BlockSpec((tb, D_MODEL), tok),
                  pl.BlockSpec((chunk * nk, D_MODEL), lambda i, c: (c, 0)),
                  pl.BlockSpec((D_MODEL, chunk * nk), lambda i, c: (0, c))]
                 + [pl.BlockSpec((H, nk, tb), tok3d)] * 4,
        out_specs=pl.BlockSpec((tb, D_MODEL), tok),
        out_shape=jax.ShapeDtypeStruct((T, D_MODEL), F32),
        scratch_shapes=[pltpu.VMEM((D_MODEL, tb), F32),
                        pltpu.VMEM((chunk * nk, tb), BF16)],
        compiler_params=_cparams("parallel", "arbitrary"),
    )(h_bf, x1, u_bf, vt_bf, r_cnt, e1, rk2, e2)


def kernel(x, norm1_w, w_in, conv_w, conv_b, dt_bias, a_log, d_skip, ssd_norm_w, sb_out_w,
           nsa_q_norm_w, nsa_k_norm_w, nsa_cmp_pos, nsa_cmp_w, nsa_out_w, w_out, norm2_w,
           peer_w_q, peer_sub_keys, peer_u, peer_v):
    B, S, _ = x.shape
    depth = w_in.shape[0]
    assert S == 2048, "NSA tiling is laid out for 2048-token sequences"
    T = B * S
    ropes = _rope_tables(S)
    xt = x.reshape(T, D_MODEL)
    for l in range(depth):
        w_in_pad = _pad_cols(w_in[l], _IN_COLMAP).astype(BF16)
        z, xbc, dt, sbq, sbk, sbv, nq, nkv, gate = _inproj(xt, norm1_w[l], w_in_pad)
        y_ssd = _ssd(z, xbc, dt, conv_w[l], conv_b[l], dt_bias[l], a_log[l], d_skip[l], ssd_norm_w[l], B, S)
        y_sb = _sb(sbq, sbk, sbv, B, S)
        q_r, kc, vc, ksw = _nsa_prep(nq, nkv, ropes, nsa_q_norm_w[l], nsa_k_norm_w[l], B, S)
        k_cmp, v_cmp = _nsa_cmp(kc, vc, nsa_cmp_pos[l], nsa_cmp_w[l], nsa_k_norm_w[l, 0], B, S)
        y_nsa = _nsa_attn(q_r, k_cmp, v_cmp, ksw, gate, nsa_out_w[l], B, S)
        w_out_pad = _pad_cols(w_out[l].T, _OUT_ROWMAP).T.astype(BF16)
        x1 = _outproj(xt, y_ssd, y_sb, y_nsa, sb_out_w[l], w_out_pad)
        wq_t = peer_w_q[l].T
        wq_t1 = wq_t.astype(BF16)
        wq_t2 = (wq_t - wq_t1.astype(F32)).astype(BF16)
        xt = _peer(x1, norm2_w[l], wq_t1, wq_t2, peer_sub_keys[l],
                   peer_u[l].astype(BF16), peer_v[l].T.astype(BF16))
    return xt.reshape(B, S, D_MODEL)
```

```python
import functools
import math

import numpy as np
import jax
import jax.numpy as jnp
from jax import lax
from jax.experimental import pallas as pl
from jax.experimental.pallas import tpu as pltpu

F32 = jnp.float32
BF16 = jnp.bfloat16

D_MODEL = 1024
HEAD_DIM = 64
SSD_WIDTH = 512
SSD_HEADS = 8
SSD_GROUPS = 2
SSD_STATE = 128
SSD_CONV = 4
SSD_CHUNK = 128
SSD_CONV_DIM = SSD_WIDTH + 2 * SSD_GROUPS * SSD_STATE
SB_WIDTH = 256
SB_HEADS = 4
NSA_WIDTH = 256
NSA_HEADS = 4
CMP_LEN = 32
CMP_STRIDE = 16
SEL_BLOCK = 64
SEL_TOPK = 16
WINDOW = 512
FORCE_SCORE = 1e9
ROT_DIM = 16
ROPE_THETA = 500000.0
PEER_HEADS = 8
N_KEYS = 128
PEER_TOPK = 16
PEER_QDIM = 256
NORM_EPS = 1e-6
NEG_BIG = -1e30

LANE = 128
SUBLANE = 8
V7X_VMEM_BYTES = 64 * 1024 * 1024
VMEM_LIMIT = V7X_VMEM_BYTES - 12 * 1024 * 1024

SLOT = LANE
Q_TILE = 128


def _cparams(*sem, flags=None):
    return pltpu.CompilerParams(dimension_semantics=sem, vmem_limit_bytes=VMEM_LIMIT, flags=flags)


def _iota(shape, dim):
    return lax.broadcasted_iota(jnp.int32, shape, dim)


def _dot(a, b):
    return jnp.dot(a.astype(BF16), b.astype(BF16), preferred_element_type=F32)


def _dot_nt(a, b):
    return lax.dot_general(a.astype(BF16), b.astype(BF16), (((1,), (1,)), ((), ())),
                           preferred_element_type=F32)


def _split(a, n):
    parts, r = [], a
    for i in range(n):
        p = r.astype(BF16)
        parts.append(p)
        if i + 1 < n:
            r = r - p.astype(F32)
    return parts


def _dot_split_lhs(a, b_exact, n=3):
    acc = None
    for p in _split(a, n):
        t = jnp.dot(p, b_exact, preferred_element_type=F32)
        acc = t if acc is None else acc + t
    return acc


def _dot_split_rhs(a_exact, b, n=3):
    acc = None
    for p in _split(b, n):
        t = jnp.dot(a_exact, p, preferred_element_type=F32)
        acc = t if acc is None else acc + t
    return acc


def _sigmoid(x):
    return 1.0 / (1.0 + jnp.exp(-x))


def _softplus(x):
    return jnp.maximum(x, 0.0) + jnp.log1p(jnp.exp(-jnp.abs(x)))


def _rms(x, w, n):
    ms = jnp.sum(x * x, axis=-1, keepdims=True) * (1.0 / n)
    return x * lax.rsqrt(ms + NORM_EPS) * w


def _softmax_rows(s):
    m = jnp.max(s, axis=-1, keepdims=True)
    e = jnp.exp(s - m)
    return e / jnp.sum(e, axis=-1, keepdims=True)


_IN_SEGS = (("z", 512, F32), ("xbc", 1024, F32), ("dt", 128, F32),
            ("sbq", 512, BF16), ("sbk", 512, BF16), ("sbv", 512, BF16),
            ("nq", 512, F32), ("nkv", 768, F32), ("gate", 128, F32))
_IN_TOTAL = sum(s[1] for s in _IN_SEGS)


def _inproj_colmap():
    src = []

    def real(a, n, pad_to=None):
        src.extend(range(a, a + n))
        if pad_to:
            src.extend([-1] * (pad_to - n))

    def heads(a, nheads):
        for h in range(nheads):
            src.extend(range(a + h * HEAD_DIM, a + (h + 1) * HEAD_DIM))
            src.extend([-1] * (SLOT - HEAD_DIM))

    o_xbc = SSD_WIDTH
    o_dt = o_xbc + SSD_CONV_DIM
    o_sb = o_dt + SSD_HEADS
    o_nq = o_sb + 3 * SB_WIDTH
    o_nkv = o_nq + NSA_WIDTH
    o_gate = o_nkv + 6 * HEAD_DIM
    real(0, SSD_WIDTH)
    real(o_xbc, SSD_CONV_DIM)
    real(o_dt, SSD_HEADS, LANE)
    heads(o_sb, SB_HEADS)
    heads(o_sb + SB_WIDTH, SB_HEADS)
    heads(o_sb + 2 * SB_WIDTH, SB_HEADS)
    heads(o_nq, NSA_HEADS)
    heads(o_nkv, 6)
    real(o_gate, 3 * NSA_HEADS, LANE)
    src = np.asarray(src, np.int32)
    assert src.shape[0] == _IN_TOTAL
    return src


_IN_COLMAP = _inproj_colmap()


def _pad_cols(w, colmap):
    g = jnp.take(w, jnp.asarray(np.maximum(colmap, 0)), axis=-1)
    return jnp.where(jnp.asarray(colmap >= 0), g, 0.0)


def _pad_heads_vec(v, nheads):
    v = v.reshape(nheads, HEAD_DIM)
    v = jnp.pad(v, ((0, 0), (0, SLOT - HEAD_DIM)))
    return v.reshape(1, nheads * SLOT)


def _inproj_kernel(x_ref, nw_ref, w_ref, *out_refs):
    x = x_ref[...]
    h = _rms(x, nw_ref[...], D_MODEL).astype(BF16)
    off = 0
    for (name, width, dt), o_ref in zip(_IN_SEGS, out_refs):
        o_ref[...] = jnp.dot(h, w_ref[:, off:off + width], preferred_element_type=F32).astype(dt)
        off += width


def _inproj(xt, norm_w, w_pad):
    T = xt.shape[0]
    tm = 256
    out_shape = [jax.ShapeDtypeStruct((T, w), dt) for (_, w, dt) in _IN_SEGS]
    out_specs = [pl.BlockSpec((tm, w), lambda i: (i, 0)) for (_, w, _) in _IN_SEGS]
    return pl.pallas_call(
        _inproj_kernel,
        grid=(T // tm,),
        in_specs=[pl.BlockSpec((tm, D_MODEL), lambda i: (i, 0)),
                  pl.BlockSpec((1, D_MODEL), lambda i: (0, 0)),
                  pl.BlockSpec((D_MODEL, _IN_TOTAL), lambda i: (0, 0))],
        out_specs=out_specs,
        out_shape=out_shape,
        compiler_params=_cparams("parallel"),
    )(xt, norm_w.reshape(1, D_MODEL), w_pad)


def _ssd_kernel(z_ref, xbc_ref, dt_ref, cw_ref, cb_ref, dtb_ref, alog_ref, dsk_ref, nw_ref,
                o_ref, xbuf, state):
    L = SSD_CHUNK
    c = pl.program_id(1)

    @pl.when(c == 0)
    def _():
        xbuf[0:SUBLANE, :] = jnp.zeros((SUBLANE, SSD_CONV_DIM), F32)
        state[...] = jnp.zeros_like(state)

    xcur = xbc_ref[...]
    xbuf[SUBLANE:SUBLANE + L, :] = xcur
    acc = jnp.broadcast_to(cb_ref[...], (L, SSD_CONV_DIM))
    for k in range(SSD_CONV):
        s0 = SUBLANE - (SSD_CONV - 1) + k
        acc = acc + xbuf[s0:s0 + L, :] * cw_ref[k:k + 1, :]
    xbuf[0:SUBLANE, :] = xcur[L - SUBLANE:L, :]
    xc = acc * _sigmoid(acc)

    xs = xc[:, 0:SSD_WIDTH]
    dt = _softplus(dt_ref[...] + dtb_ref[...])
    a_dec = dt * (-jnp.exp(alog_ref[...]))

    row = _iota((L, L), 0)
    col = _iota((L, L), 1)
    tril = row >= col
    tri = tril.astype(BF16)
    cs = _dot_split_rhs(tri, a_dec)
    e64 = (_iota((LANE, SSD_WIDTH), 0) == _iota((LANE, SSD_WIDTH), 1) // HEAD_DIM).astype(BF16)
    e128 = (_iota((LANE, SSD_HEADS * LANE), 0) == _iota((LANE, SSD_HEADS * LANE), 1) // LANE).astype(BF16)
    dt_e = _dot_split_lhs(dt, e64)
    cs_e = _dot_split_lhs(cs, e64)
    cs_e2 = _dot_split_lhs(cs, e128)
    cs_t = cs.T
    cs_last = cs_e[L - 1:L, :]
    x_dt = xs * dt_e
    x_dec = x_dt * jnp.exp(cs_last - cs_e)
    in_dec = jnp.exp(cs_e)
    chunk_dec = jnp.exp(cs_last)
    lane_lo = _iota((L, LANE), 1) < HEAD_DIM

    ys = []
    for g in range(SSD_GROUPS):
        b_g = xc[:, SSD_WIDTH + g * SSD_STATE:SSD_WIDTH + (g + 1) * SSD_STATE]
        c_g = xc[:, SSD_WIDTH + (SSD_GROUPS + g) * SSD_STATE:SSD_WIDTH + (SSD_GROUPS + g + 1) * SSD_STATE]
        cb = _dot_nt(c_g, b_g)
        b_gt = b_g.T
        for m in range(2 * g, 2 * g + 2):
            lanes = slice(m * LANE, (m + 1) * LANE)
            xp = x_dt[:, lanes]
            yd = []
            for h in (2 * m, 2 * m + 1):
                diff = cs_e2[:, h * LANE:(h + 1) * LANE] - cs_t[h:h + 1, :]
                lm = jnp.exp(jnp.where(tril, diff, -jnp.inf))
                yd.append(_dot(lm * cb, xp))
            y_diag = jnp.where(lane_lo, yd[0], yd[1])
            s_prev = state[m]
            y_off = _dot(c_g, s_prev) * in_dec[:, lanes]
            state[m] = s_prev * chunk_dec[:, lanes] + _dot(b_gt, x_dec[:, lanes])
            ys.append(y_diag + y_off + xs[:, lanes] * dsk_ref[:, lanes])
    y = jnp.concatenate(ys, axis=1)
    zt = z_ref[...]
    o_ref[...] = _rms(y * (zt * _sigmoid(zt)), nw_ref[...], SSD_WIDTH)


def _ssd(z, xbc, dt, conv_w, conv_b, dt_bias, a_log, d_skip, norm_w, B, S):
    T = B * S
    L = SSD_CHUNK
    nc = S // L
    pad8 = lambda v: jnp.pad(v, (0, LANE - SSD_HEADS)).reshape(1, LANE)
    row = lambda b, c: (b * nc + c, 0)
    fixed = lambda b, c: (0, 0)
    return pl.pallas_call(
        _ssd_kernel,
        grid=(B, nc),
        in_specs=[pl.BlockSpec((L, SSD_WIDTH), row),
                  pl.BlockSpec((L, SSD_CONV_DIM), row),
                  pl.BlockSpec((L, LANE), row),
                  pl.BlockSpec((SSD_CONV, SSD_CONV_DIM), fixed),
                  pl.BlockSpec((1, SSD_CONV_DIM), fixed),
                  pl.BlockSpec((1, LANE), fixed),
                  pl.BlockSpec((1, LANE), fixed),
                  pl.BlockSpec((1, SSD_WIDTH), fixed),
                  pl.BlockSpec((1, SSD_WIDTH), fixed)],
        out_specs=pl.BlockSpec((L, SSD_WIDTH), row),
        out_shape=jax.ShapeDtypeStruct((T, SSD_WIDTH), F32),
        scratch_shapes=[pltpu.VMEM((SUBLANE + L, SSD_CONV_DIM), F32),
                        pltpu.VMEM((SSD_HEADS // 2, SSD_STATE, LANE), F32)],
        compiler_params=_cparams("parallel", "arbitrary"),
    )(z, xbc, dt, conv_w, conv_b.reshape(1, -1), pad8(dt_bias), pad8(a_log),
      jnp.repeat(d_skip, HEAD_DIM).reshape(1, SSD_WIDTH), norm_w.reshape(1, SSD_WIDTH))


SB_LOG_CUTOFF = -104.0


def _sb_kernel(q_ref, k_ref, v_ref, o_ref, cf_ref):
    tq = Q_TILE
    H = SB_HEADS
    qi = pl.program_id(1)
    row = _iota((tq, tq), 0)
    col = _iota((tq, tq), 1)
    strict = col < row
    r2 = _iota((tq, 2 * tq), 0)
    c2 = _iota((tq, 2 * tq), 1)
    umat = ((r2 > c2) | (c2 >= tq)).astype(BF16)
    scale = HEAD_DIM ** -0.5
    o_ref[...] = jnp.zeros_like(o_ref)
    cf_ref[...] = jnp.zeros_like(cf_ref)

    def cond(c):
        i, live = c
        return jnp.logical_and(i <= qi, live > 0)

    def body(c):
        i, _ = c
        ks = pl.multiple_of((qi - i) * tq, tq)
        mask = jnp.logical_or(i > 0, strict)[None]
        head_lanes = [slice(h * SLOT, (h + 1) * SLOT) for h in range(H)]
        z = jnp.concatenate([_dot_nt(q_ref[:, ln], k_ref[pl.ds(ks, tq), ln]) for ln in head_lanes], axis=0)
        z = (z * scale).reshape(H, tq, tq)
        l1p = jnp.log1p(jnp.exp(-jnp.abs(z)))
        log_beta = jnp.where(mask, jnp.minimum(z, 0.0) - l1p, -jnp.inf).reshape(H * tq, tq)
        log_stay = jnp.where(mask, -jnp.maximum(z, 0.0) - l1p, 0.0).reshape(H * tq, tq)
        r = _dot_split_lhs(log_stay, umat, 2)
        cf = cf_ref[...]
        w = jnp.exp(log_beta + r[:, 0:tq] + cf)
        for h, ln in enumerate(head_lanes):
            o_ref[:, ln] += _dot(w[h * tq:(h + 1) * tq, :], v_ref[pl.ds(ks, tq), ln])
        cf = cf + r[:, tq:2 * tq]
        cf_ref[...] = cf
        live = (jnp.max(cf) > SB_LOG_CUTOFF).astype(jnp.int32)
        return i + 1, live

    lax.while_loop(cond, body, (jnp.int32(0), jnp.int32(1)))


def _sb(sbq, sbk, sbv, B, S):
    T = B * S
    nq = S // Q_TILE
    W = SB_HEADS * SLOT
    return pl.pallas_call(
        _sb_kernel,
        grid=(B, nq),
        in_specs=[pl.BlockSpec((Q_TILE, W), lambda b, i: (b * nq + i, 0)),
                  pl.BlockSpec((S, W), lambda b, i: (b, 0)),
                  pl.BlockSpec((S, W), lambda b, i: (b, 0))],
        out_specs=pl.BlockSpec((Q_TILE, W), lambda b, i: (b * nq + i, 0)),
        out_shape=jax.ShapeDtypeStruct((T, W), F32),
        scratch_shapes=[pltpu.VMEM((SB_HEADS * Q_TILE, SLOT), F32)],
        compiler_params=_cparams("parallel", "arbitrary"),
    )(sbq, sbk, sbv)


def _rope_tables(S):
    pos = jnp.arange(S, dtype=F32)
    inv_freq = ROPE_THETA ** (-jnp.arange(0, ROT_DIM, 2, dtype=F32) / ROT_DIM)
    ang = pos[:, None] * inv_freq[None, :]
    cos, sin = jnp.cos(ang), jnp.sin(ang)
    half = ROT_DIM // 2
    ones = jnp.ones((S, SLOT - ROT_DIM), F32)
    zeros_h = jnp.zeros((S, half), F32)
    zeros_r = jnp.zeros((S, SLOT - ROT_DIM), F32)
    cosf = jnp.concatenate([cos, cos, ones], axis=1)
    sin_a = jnp.concatenate([-sin, zeros_h, zeros_r], axis=1)
    sin_b = jnp.concatenate([zeros_h, sin, zeros_r], axis=1)
    return cosf, sin_a, sin_b


def _nsa_prep_kernel(q_ref, kv_ref, cos_ref, sa_ref, sb_ref, qw_ref, kw_ref,
                     qo_ref, kc_ref, vc_ref, ksw_ref):
    cos, sa, sb = cos_ref[...], sa_ref[...], sb_ref[...]
    half = ROT_DIM // 2

    def rope(x):
        return x * cos + pltpu.roll(x, SLOT - half, 1) * sa + pltpu.roll(x, half, 1) * sb

    qw = qw_ref[...]
    for h in range(NSA_HEADS):
        lanes = slice(h * SLOT, (h + 1) * SLOT)
        qo_ref[:, lanes] = rope(_rms(q_ref[:, lanes], qw, HEAD_DIM)).astype(BF16)
    kc_ref[...] = rope(kv_ref[:, 0:SLOT])
    vc_ref[...] = kv_ref[:, SLOT:2 * SLOT]
    ksw_ref[:, 0:SLOT] = rope(_rms(kv_ref[:, 2 * SLOT:3 * SLOT], kw_ref[1:2, :], HEAD_DIM)).astype(BF16)
    ksw_ref[:, SLOT:2 * SLOT] = kv_ref[:, 3 * SLOT:4 * SLOT].astype(BF16)
    ksw_ref[:, 2 * SLOT:3 * SLOT] = rope(_rms(kv_ref[:, 4 * SLOT:5 * SLOT], kw_ref[2:3, :], HEAD_DIM)).astype(BF16)
    ksw_ref[:, 3 * SLOT:4 * SLOT] = kv_ref[:, 5 * SLOT:6 * SLOT].astype(BF16)


def _nsa_prep(nq, nkv, ropes, q_norm_w, k_norm_w, B, S):
    T = B * S
    ts = 256
    nt = S // ts
    cosf, sin_a, sin_b = ropes
    pad = lambda v: jnp.pad(v, ((0, 0), (0, SLOT - HEAD_DIM)))
    row = lambda b, i: (b * nt + i, 0)
    pos = lambda b, i: (i, 0)
    fixed = lambda b, i: (0, 0)
    return pl.pallas_call(
        _nsa_prep_kernel,
        grid=(B, nt),
        in_specs=[pl.BlockSpec((ts, NSA_HEADS * SLOT), row),
                  pl.BlockSpec((ts, 6 * SLOT), row),
                  pl.BlockSpec((ts, SLOT), pos),
                  pl.BlockSpec((ts, SLOT), pos),
                  pl.BlockSpec((ts, SLOT), pos),
                  pl.BlockSpec((1, SLOT), fixed),
                  pl.BlockSpec((3, SLOT), fixed)],
        out_specs=[pl.BlockSpec((ts, NSA_HEADS * SLOT), row),
                   pl.BlockSpec((ts, SLOT), row),
                   pl.BlockSpec((ts, SLOT), row),
                   pl.BlockSpec((ts, 4 * SLOT), row)],
        out_shape=[jax.ShapeDtypeStruct((T, NSA_HEADS * SLOT), BF16),
                   jax.ShapeDtypeStruct((T, SLOT), F32),
                   jax.ShapeDtypeStruct((T, SLOT), F32),
                   jax.ShapeDtypeStruct((T, 4 * SLOT), BF16)],
        compiler_params=_cparams("parallel", "parallel"),
    )(nq, nkv, cosf, sin_a, sin_b, pad(q_norm_w.reshape(1, HEAD_DIM)), pad(k_norm_w))


def _nsa_cmp_kernel(kc_ref, vc_ref, pos_ref, w_ref, knw_ref, ko_ref, vo_ref):
    nseg = ko_ref.shape[0]
    half = CMP_LEN // 2
    for which, (src, dst) in enumerate(((kc_ref, ko_ref), (vc_ref, vo_ref))):
        f1 = jnp.zeros((nseg, SLOT), F32)
        f2 = jnp.zeros((nseg, SLOT), F32)
        for p in range(half):
            rows = src[pl.ds(p, nseg, stride=CMP_STRIDE), :]
            f1 = f1 + _dot(rows + pos_ref[which, p:p + 1, :], w_ref[which, p])
            f2 = f2 + _dot(rows + pos_ref[which, half + p:half + p + 1, :], w_ref[which, half + p])
        pre = f1 + pltpu.roll(f2, nseg - 1, 0)
        if which == 0:
            pre = _rms(pre, knw_ref[...], HEAD_DIM)
        dst[...] = pre


def _nsa_cmp(kc, vc, cmp_pos, cmp_w, k_norm_w0, B, S):
    nseg = S // CMP_STRIDE
    pos = jnp.pad(cmp_pos, ((0, 0), (0, 0), (0, SLOT - HEAD_DIM)))
    w = cmp_w.reshape(2, CMP_LEN, HEAD_DIM, HEAD_DIM)
    w = jnp.pad(w, ((0, 0), (0, 0), (0, SLOT - HEAD_DIM), (0, SLOT - HEAD_DIM))).astype(BF16)
    knw = jnp.pad(k_norm_w0.reshape(1, HEAD_DIM), ((0, 0), (0, SLOT - HEAD_DIM)))
    per_b = lambda b: (b, 0)
    return pl.pallas_call(
        _nsa_cmp_kernel,
        grid=(B,),
        in_specs=[pl.BlockSpec((S, SLOT), per_b),
                  pl.BlockSpec((S, SLOT), per_b),
                  pl.BlockSpec((2, CMP_LEN, SLOT), lambda b: (0, 0, 0)),
                  pl.BlockSpec((2, CMP_LEN, SLOT, SLOT), lambda b: (0, 0, 0, 0)),
                  pl.BlockSpec((1, SLOT), lambda b: (0, 0))],
        out_specs=[pl.BlockSpec((nseg, SLOT), per_b), pl.BlockSpec((nseg, SLOT), per_b)],
        out_shape=[jax.ShapeDtypeStruct((B * nseg, SLOT), F32),
                   jax.ShapeDtypeStruct((B * nseg, SLOT), F32)],
        compiler_params=_cparams("parallel"),
    )(kc, vc, pos, w, knw)


def _nsa_attn_kernel(q_ref, kc_ref, vc_ref, ksw_ref, g_ref, ow_ref, o_ref, *, S):
    tq = Q_TILE
    n_cmp_pad = S // CMP_STRIDE
    n_sel = S // SEL_BLOCK
    n_top = min(SEL_TOPK, n_sel)
    n_win = WINDOW + tq
    scale = HEAD_DIM ** -0.5
    qi = pl.program_id(1)
    t0 = qi * tq

    t_c = t0 + _iota((tq, n_cmp_pad), 0)
    c_c = _iota((tq, n_cmp_pad), 1)
    vis = (c_c * CMP_STRIDE + (CMP_LEN - 1)) <= t_c
    kc = kc_ref[...]
    vc = vc_ref[...]
    H = NSA_HEADS
    q4 = jnp.concatenate([q_ref[:, h * SLOT:(h + 1) * SLOT] for h in range(H)], axis=0)

    def attend(k, v, ok, nkeys):
        s = (_dot_nt(q4, k) * scale).reshape(H, tq, nkeys)
        p = _softmax_rows(jnp.where(ok[None], s, NEG_BIG))
        return _dot(p.reshape(H * tq, nkeys), v)

    s = (_dot_nt(q4, kc) * scale).reshape(H, tq, n_cmp_pad)
    p = jnp.where(vis[None], _softmax_rows(jnp.where(vis[None], s, NEG_BIG)), 0.0)
    o_cmp = _dot(p.reshape(H * tq, n_cmp_pad), vc)
    p_sum = jnp.sum(p, axis=0)

    r_o = _iota((n_cmp_pad, LANE), 0) * CMP_STRIDE
    j_o = _iota((n_cmp_pad, LANE), 1) * SEL_BLOCK
    overlap = ((r_o < j_o + SEL_BLOCK) & (r_o + CMP_LEN > j_o)
               & (_iota((n_cmp_pad, LANE), 0) < n_cmp_pad - 1)).astype(BF16)
    imp_t = _dot_split_lhs(p_sum, overlap).T
    j_t = _iota((LANE, tq), 0)
    t_t = t0 + _iota((LANE, tq), 1)
    blk = t_t // SEL_BLOCK
    forced = (j_t == 0) | (j_t == blk) | (j_t == blk - 1)
    causal = (j_t * SEL_BLOCK <= t_t) & (j_t < n_sel)
    val = jnp.where(causal, jnp.where(forced, FORCE_SCORE, imp_t), -jnp.inf)
    rank = jnp.zeros((LANE, tq), jnp.int32)
    for jp in range(n_sel):
        vj = val[jp:jp + 1, :]
        beats = (vj > val) | ((vj == val) & (j_t > jp))
        rank = rank + beats.astype(jnp.int32)
    sel_t = (causal & (rank < n_top)).astype(F32)
    sel = sel_t.T.astype(BF16)
    expand = (_iota((LANE, S), 0) == _iota((LANE, S), 1) // SEL_BLOCK).astype(BF16)
    sel_keys = jnp.dot(sel, expand, preferred_element_type=F32)

    t_s = t0 + _iota((tq, S), 0)
    k_s = _iota((tq, S), 1)
    ok_sel = (sel_keys > 0.5) & (k_s <= t_s)
    o_sel = attend(ksw_ref[:, 0:SLOT], ksw_ref[:, SLOT:2 * SLOT], ok_sel, S)

    start = pl.multiple_of(jnp.clip(t0 - WINDOW, 0, S - n_win), tq)
    k_win = ksw_ref[pl.ds(start, n_win), 2 * SLOT:3 * SLOT]
    v_win = ksw_ref[pl.ds(start, n_win), 3 * SLOT:4 * SLOT]
    rel = (t0 + _iota((tq, n_win), 0)) - (start + _iota((tq, n_win), 1))
    ok_win = (rel >= 0) & (rel < WINDOW)
    o_win = attend(k_win, v_win, ok_win, n_win)

    gate = _sigmoid(g_ref[...])
    n_g = 3 * NSA_HEADS
    e_g = (_iota((LANE, n_g * SLOT), 0) == _iota((LANE, n_g * SLOT), 1) // SLOT).astype(BF16)
    g_e = _dot_split_lhs(gate, e_g)
    outs = []
    for h in range(NSA_HEADS):
        g0 = g_e[:, (3 * h) * SLOT:(3 * h + 1) * SLOT]
        g1 = g_e[:, (3 * h + 1) * SLOT:(3 * h + 2) * SLOT]
        g2 = g_e[:, (3 * h + 2) * SLOT:(3 * h + 3) * SLOT]
        rows = slice(h * tq, (h + 1) * tq)
        outs.append(g0 * o_cmp[rows, :] + g1 * o_sel[rows, :] + g2 * o_win[rows, :])
    o = jnp.concatenate(outs, axis=1)
    o_ref[...] = _rms(o, ow_ref[...], NSA_WIDTH)


def _nsa_attn(q_r, k_cmp, v_cmp, ksw, gate, out_w, B, S):
    T = B * S
    nq = S // Q_TILE
    nseg = S // CMP_STRIDE
    row = lambda b, i: (b * nq + i, 0)
    per_b = lambda b, i: (b, 0)
    return pl.pallas_call(
        functools.partial(_nsa_attn_kernel, S=S),
        grid=(B, nq),
        in_specs=[pl.BlockSpec((Q_TILE, NSA_HEADS * SLOT), row),
                  pl.BlockSpec((nseg, SLOT), per_b),
                  pl.BlockSpec((nseg, SLOT), per_b),
                  pl.BlockSpec((S, 4 * SLOT), per_b),
                  pl.BlockSpec((Q_TILE, LANE), row),
                  pl.BlockSpec((1, NSA_HEADS * SLOT), lambda b, i: (0, 0))],
        out_specs=pl.BlockSpec((Q_TILE, NSA_HEADS * SLOT), row),
        out_shape=jax.ShapeDtypeStruct((T, NSA_HEADS * SLOT), F32),
        compiler_params=_cparams("parallel", "parallel"),
    )(q_r, k_cmp, v_cmp, ksw, gate, _pad_heads_vec(out_w, NSA_HEADS))


def _outproj_kernel(x_ref, yssd_ref, ysb_ref, ynsa_ref, sbw_ref, w_ref, o_ref):
    ysb = _rms(ysb_ref[...], sbw_ref[...], SB_WIDTH)
    w0 = SSD_WIDTH
    w1 = w0 + SB_HEADS * SLOT
    w2 = w1 + NSA_HEADS * SLOT
    acc = x_ref[...]
    acc = acc + jnp.dot(yssd_ref[...].astype(BF16), w_ref[0:w0, :], preferred_element_type=F32)
    acc = acc + jnp.dot(ysb.astype(BF16), w_ref[w0:w1, :], preferred_element_type=F32)
    acc = acc + jnp.dot(ynsa_ref[...].astype(BF16), w_ref[w1:w2, :], preferred_element_type=F32)
    o_ref[...] = acc


def _outproj_rowmap():
    src = list(range(SSD_WIDTH))
    for base, nheads in ((SSD_WIDTH, SB_HEADS), (SSD_WIDTH + SB_WIDTH, NSA_HEADS)):
        for h in range(nheads):
            src.extend(range(base + h * HEAD_DIM, base + (h + 1) * HEAD_DIM))
            src.extend([-1] * (SLOT - HEAD_DIM))
    return np.asarray(src, np.int32)


_OUT_ROWMAP = _outproj_rowmap()


def _outproj(xt, y_ssd, y_sb, y_nsa, sb_out_w, w_pad):
    T = xt.shape[0]
    tm = 256
    kdim = _OUT_ROWMAP.shape[0]
    row = lambda i: (i, 0)
    fixed = lambda i: (0, 0)
    return pl.pallas_call(
        _outproj_kernel,
        grid=(T // tm,),
        in_specs=[pl.BlockSpec((tm, D_MODEL), row),
                  pl.BlockSpec((tm, SSD_WIDTH), row),
                  pl.BlockSpec((tm, SB_HEADS * SLOT), row),
                  pl.BlockSpec((tm, NSA_HEADS * SLOT), row),
                  pl.BlockSpec((1, SB_HEADS * SLOT), fixed),
                  pl.BlockSpec((kdim, D_MODEL), fixed)],
        out_specs=pl.BlockSpec((tm, D_MODEL), row),
        out_shape=jax.ShapeDtypeStruct((T, D_MODEL), F32),
        compiler_params=_cparams("parallel"),
    )(xt, y_ssd, y_sb, y_nsa, _pad_heads_vec(sb_out_w, SB_HEADS), w_pad)


def _cmp_exchange(x, y):
    if y is None:
        return x, None
    if x is None:
        return y, None
    return jnp.maximum(x, y), jnp.minimum(x, y)


def _bitonic_sort_desc(u):
    u = list(u)
    n = len(u)
    k = 2
    while k <= n:
        j = k // 2
        while j >= 1:
            for i in range(n):
                l = i ^ j
                if l > i:
                    hi, lo = _cmp_exchange(u[i], u[l])
                    u[i], u[l] = (hi, lo) if (i & k) == 0 else (lo, hi)
            j //= 2
        k *= 2
    return u


def _merge_top(a, b):
    n = len(a)
    c = [_cmp_exchange(a[i], b[n - 1 - i])[0] for i in range(n)]
    d = n // 2
    while d >= 1:
        for i in range(n):
            if (i & d) == 0:
                c[i], c[i + d] = _cmp_exchange(c[i], c[i + d])
        d //= 2
    return c


def _top16_replicated(s):
    groups = s.shape[0] // SUBLANE
    u = _bitonic_sort_desc([s[g * SUBLANE:(g + 1) * SUBLANE, :] for g in range(groups)])
    shift = SUBLANE // 2
    while shift >= 1:
        u = _merge_top(u, [pltpu.roll(x, shift, 0) for x in u])
        shift //= 2
    return u


def _bf16_pair_words(x):
    bits = lax.bitcast_convert_type(x.astype(BF16).astype(F32), jnp.uint32)
    return bits | (bits >> 16)


def _peer_select_kernel(x_ref, nw_ref, wq_ref, sk_ref,
                        h_ref, r_ref, e1_ref, rk_ref, e2_ref):
    K = PEER_TOPK
    half = PEER_QDIM // 2
    nk = N_KEYS
    h = _rms(x_ref[...], nw_ref[...], D_MODEL)
    h_ref[...] = h.T.astype(BF16)
    q_t = _dot_nt(wq_ref[...], h)
    groups = nk // SUBLANE
    tokens = q_t.shape[1]
    for hd, t0 in [(hd, t0) for hd in range(PEER_HEADS) for t0 in range(0, tokens, LANE)]:
        lanes = slice(t0, t0 + LANE)
        r0 = hd * 2 * half
        if t0 == 0:
            s1_all = _dot(sk_ref[0], q_t[r0:r0 + half, :])
            s2_all = _dot(sk_ref[1], q_t[r0 + half:r0 + 2 * half, :])
        s1, s2 = s1_all[:, lanes], s2_all[:, lanes]
        sv1 = _top16_replicated(s1)
        sv2 = _top16_replicated(s2)
        lists = []
        for a in range(K):
            row = [sv1[a] + sv2[b] for b in range(K // (a + 1))]
            lists.append(row + [None] * (K - len(row)))
        while len(lists) > 1:
            nxt = [_merge_top(lists[i], lists[i + 1]) for i in range(0, len(lists) - 1, 2)]
            if len(lists) % 2:
                nxt.append(lists[-1])
            lists = nxt
        cv = lists[0]
        tau = cv[K - 1]
        zsum = None
        for k in range(K):
            e = jnp.exp(cv[k] - cv[0])
            zsum = e if zsum is None else zsum + e
        inv_z = 1.0 / zsum
        r_parts, rk_parts, e1_parts, e2_parts = [], [], [], []
        for g in range(groups):
            rows = slice(g * SUBLANE, (g + 1) * SUBLANE)
            a1, a2 = s1[rows, :], s2[rows, :]
            cnt = jnp.full(a1.shape, -1.0, F32)
            rk = jnp.zeros(a2.shape, F32)
            for b in range(K):
                cnt = cnt + jnp.where(a1 + sv2[b] >= tau, 1.0, 0.0)
                rk = rk + jnp.where(sv2[b] > a2, 1.0, 0.0)
            r_parts.append(cnt)
            rk_parts.append(rk)
            e1_parts.append(jnp.exp(a1 - sv1[0]))
            e2_parts.append(jnp.exp(a2 - sv2[0]) * inv_z)
        r_ref[hd, :, lanes] = _bf16_pair_words(jnp.concatenate(r_parts, axis=0))
        e1_ref[hd, :, lanes] = _bf16_pair_words(jnp.concatenate(e1_parts, axis=0))
        rk_ref[hd, :, lanes] = jnp.concatenate(rk_parts, axis=0).astype(BF16)
        e2_ref[hd, :, lanes] = jnp.concatenate(e2_parts, axis=0).astype(BF16)


BF16_ROWS = 2 * SUBLANE
GATE_LANES = 2 * LANE


def _peer_dense_kernel(h_ref, x_ref, u_ref, vt_ref, r_ref, e1_ref, rk_ref, e2_ref,
                       o_ref, acc_ref, p_ref):
    c = pl.program_id(1)
    nk = N_KEYS
    tb = h_ref.shape[1]

    @pl.when(c == 0)
    def _():
        acc_ref[...] = jnp.zeros_like(acc_ref)

    h_t = jnp.dot(u_ref[...], h_ref[...], preferred_element_type=F32)
    row0 = pl.multiple_of(c * SUBLANE, SUBLANE)
    r_rows = [r_ref[hd, pl.ds(row0, SUBLANE), :] for hd in range(PEER_HEADS)]
    e1_rows = [e1_ref[hd, pl.ds(row0, SUBLANE), :] for hd in range(PEER_HEADS)]
    tw = GATE_LANES
    for il in range(SUBLANE):
        for t0 in range(0, tb, tw):
            lanes = slice(t0, t0 + tw)
            a = h_t[il * nk:(il + 1) * nk, lanes]
            act = (0.5 * a * (1.0 + lax.erf(a * (2.0 ** -0.5)))).astype(BF16)
            rb = [pltpu.bitcast(jnp.broadcast_to(r_rows[hd][il:il + 1, lanes], (SUBLANE, tw)), BF16)
                  for hd in range(PEER_HEADS)]
            eb = [pltpu.bitcast(jnp.broadcast_to(e1_rows[hd][il:il + 1, lanes], (SUBLANE, tw)), BF16)
                  for hd in range(PEER_HEADS)]
            for j in range(nk // BF16_ROWS):
                rows = pl.ds(j * BF16_ROWS, BF16_ROWS)
                w = None
                for hd in range(PEER_HEADS):
                    t = jnp.where(rk_ref[hd, rows, lanes] <= rb[hd], e2_ref[hd, rows, lanes],
                                  jnp.zeros((), BF16)) * eb[hd]
                    w = t if w is None else w + t
                p_ref[pl.ds(il * nk + j * BF16_ROWS, BF16_ROWS), lanes] = (
                    w * act[j * BF16_ROWS:(j + 1) * BF16_ROWS, :])
    acc_ref[...] += jnp.dot(vt_ref[...], p_ref[...], preferred_element_type=F32)

    @pl.when(c == pl.num_programs(1) - 1)
    def _():
        o_ref[...] = x_ref[...] + acc_ref[...].T


def _peer(x1, norm_w, wq_t, sub_keys, u_bf, vt_bf):
    T = x1.shape[0]
    H, nk = PEER_HEADS, N_KEYS
    tb_s = 256
    row = lambda i: (i, 0)
    fixed2 = lambda i: (0, 0)
    tok3 = lambda i: (0, 0, i)
    head_f32 = jax.ShapeDtypeStruct((H, nk, T), jnp.uint32)
    head_bf16 = jax.ShapeDtypeStruct((H, nk, T), BF16)
    h_bf, r_cnt, e1, rk2, e2 = pl.pallas_call(
        _peer_select_kernel,
        grid=(T // tb_s,),
        in_specs=[pl.BlockSpec((tb_s, D_MODEL), row),
                  pl.BlockSpec((1, D_MODEL), fixed2),
                  pl.BlockSpec((H * PEER_QDIM, D_MODEL), fixed2),
                  pl.BlockSpec((2, nk, PEER_QDIM // 2), lambda i: (0, 0, 0))],
        out_specs=[pl.BlockSpec((D_MODEL, tb_s), lambda i: (0, i))] + [pl.BlockSpec((H, nk, tb_s), tok3)] * 4,
        out_shape=[jax.ShapeDtypeStruct((D_MODEL, T), BF16), head_f32, head_f32, head_bf16, head_bf16],
        compiler_params=_cparams("parallel"),
    )(x1, norm_w.reshape(1, D_MODEL), wq_t, sub_keys)

    tb = 512
    chunk = SUBLANE
    gate_spec = pl.BlockSpec((H, nk, tb), lambda i, c: (0, 0, i))
    return pl.pallas_call(
        _peer_dense_kernel,
        grid=(T // tb, nk // chunk),
        in_specs=[pl.BlockSpec((D_MODEL, tb), lambda i, c: (0, i)),
                  pl.BlockSpec((tb, D_MODEL), lambda i, c: (i, 0)),
                  pl.BlockSpec((chunk * nk, D_MODEL), lambda i, c: (c, 0)),
                  pl.BlockSpec((D_MODEL, chunk * nk), lambda i, c: (0, c))]
                 + [gate_spec] * 4,
        out_specs=pl.BlockSpec((tb, D_MODEL), lambda i, c: (i, 0)),
        out_shape=jax.ShapeDtypeStruct((T, D_MODEL), F32),
        scratch_shapes=[pltpu.VMEM((D_MODEL, tb), F32),
                        pltpu.VMEM((chunk * nk, tb), BF16)],
        compiler_params=_cparams("parallel", "arbitrary"),
    )(h_bf, x1, u_bf, vt_bf, r_cnt, e1, rk2, e2)


def kernel(x, norm1_w, w_in, conv_w, conv_b, dt_bias, a_log, d_skip, ssd_norm_w, sb_out_w,
           nsa_q_norm_w, nsa_k_norm_w, nsa_cmp_pos, nsa_cmp_w, nsa_out_w, w_out, norm2_w,
           peer_w_q, peer_sub_keys, peer_u, peer_v):
    B, S, _ = x.shape
    depth = w_in.shape[0]
    assert S == 2048, "NSA tiling is laid out for 2048-token sequences"
    T = B * S
    ropes = _rope_tables(S)
    xt = x.reshape(T, D_MODEL)
    for l in range(depth):
        w_in_pad = _pad_cols(w_in[l], _IN_COLMAP).astype(BF16)
        z, xbc, dt, sbq, sbk, sbv, nq, nkv, gate = _inproj(xt, norm1_w[l], w_in_pad)
        y_ssd = _ssd(z, xbc, dt, conv_w[l], conv_b[l], dt_bias[l], a_log[l], d_skip[l], ssd_norm_w[l], B, S)
        y_sb = _sb(sbq, sbk, sbv, B, S)
        q_r, kc, vc, ksw = _nsa_prep(nq, nkv, ropes, nsa_q_norm_w[l], nsa_k_norm_w[l], B, S)
        k_cmp, v_cmp = _nsa_cmp(kc, vc, nsa_cmp_pos[l], nsa_cmp_w[l], nsa_k_norm_w[l, 0], B, S)
        y_nsa = _nsa_attn(q_r, k_cmp, v_cmp, ksw, gate, nsa_out_w[l], B, S)
        w_out_pad = _pad_cols(w_out[l].T, _OUT_ROWMAP).T.astype(BF16)
        x1 = _outproj(xt, y_ssd, y_sb, y_nsa, sb_out_w[l], w_out_pad)
        xt = _peer(x1, norm2_w[l], peer_w_q[l].T.astype(BF16), peer_sub_keys[l],
                   peer_u[l].astype(BF16), peer_v[l].T.astype(BF16))
    return xt.reshape(B, S, D_MODEL)
```

```python
import functools
import math

import numpy as np
import jax
import jax.numpy as jnp
from jax import lax
from jax.experimental import pallas as pl
from jax.experimental.pallas import tpu as pltpu

F32 = jnp.float32
BF16 = jnp.bfloat16

D_MODEL = 1024
HEAD_DIM = 64
SSD_WIDTH = 512
SSD_HEADS = 8
SSD_GROUPS = 2
SSD_STATE = 128
SSD_CONV = 4
SSD_CHUNK = 128
SSD_CONV_DIM = SSD_WIDTH + 2 * SSD_GROUPS * SSD_STATE
SB_WIDTH = 256
SB_HEADS = 4
NSA_WIDTH = 256
NSA_HEADS = 4
CMP_LEN = 32
CMP_STRIDE = 16
SEL_BLOCK = 64
SEL_TOPK = 16
WINDOW = 512
FORCE_SCORE = 1e9
ROT_DIM = 16
ROPE_THETA = 500000.0
PEER_HEADS = 8
N_KEYS = 128
PEER_TOPK = 16
PEER_QDIM = 256
NORM_EPS = 1e-6
NEG_BIG = -1e30

LANE = 128
SUBLANE = 8
V7X_VMEM_BYTES = 64 * 1024 * 1024
VMEM_LIMIT = V7X_VMEM_BYTES - 12 * 1024 * 1024

SLOT = LANE
Q_TILE = 128


def _cparams(*sem, flags=None):
    return pltpu.CompilerParams(dimension_semantics=sem, vmem_limit_bytes=VMEM_LIMIT, flags=flags)


def _iota(shape, dim):
    return lax.broadcasted_iota(jnp.int32, shape, dim)


def _dot(a, b):
    return jnp.dot(a.astype(BF16), b.astype(BF16), preferred_element_type=F32)


def _dot_nt(a, b):
    return lax.dot_general(a.astype(BF16), b.astype(BF16), (((1,), (1,)), ((), ())),
                           preferred_element_type=F32)


def _split(a, n):
    parts, r = [], a
    for i in range(n):
        p = r.astype(BF16)
        parts.append(p)
        if i + 1 < n:
            r = r - p.astype(F32)
    return parts


def _dot_split_lhs(a, b_exact, n=3):
    acc = None
    for p in _split(a, n):
        t = jnp.dot(p, b_exact, preferred_element_type=F32)
        acc = t if acc is None else acc + t
    return acc


def _dot_split_rhs(a_exact, b, n=3):
    acc = None
    for p in _split(b, n):
        t = jnp.dot(a_exact, p, preferred_element_type=F32)
        acc = t if acc is None else acc + t
    return acc


def _sigmoid(x):
    return 1.0 / (1.0 + jnp.exp(-x))


def _softplus(x):
    return jnp.maximum(x, 0.0) + jnp.log1p(jnp.exp(-jnp.abs(x)))


def _rms(x, w, n):
    ms = jnp.sum(x * x, axis=-1, keepdims=True) * (1.0 / n)
    return x * lax.rsqrt(ms + NORM_EPS) * w


def _softmax_rows(s):
    m = jnp.max(s, axis=-1, keepdims=True)
    e = jnp.exp(s - m)
    return e / jnp.sum(e, axis=-1, keepdims=True)


_IN_SEGS = (("z", 512, F32), ("xbc", 1024, F32), ("dt", 128, F32),
            ("sbq", 512, BF16), ("sbk", 512, BF16), ("sbv", 512, BF16),
            ("nq", 512, F32), ("nkv", 768, F32), ("gate", 128, F32))
_IN_TOTAL = sum(s[1] for s in _IN_SEGS)


def _inproj_colmap():
    src = []

    def real(a, n, pad_to=None):
        src.extend(range(a, a + n))
        if pad_to:
            src.extend([-1] * (pad_to - n))

    def heads(a, nheads):
        for h in range(nheads):
            src.extend(range(a + h * HEAD_DIM, a + (h + 1) * HEAD_DIM))
            src.extend([-1] * (SLOT - HEAD_DIM))

    o_xbc = SSD_WIDTH
    o_dt = o_xbc + SSD_CONV_DIM
    o_sb = o_dt + SSD_HEADS
    o_nq = o_sb + 3 * SB_WIDTH
    o_nkv = o_nq + NSA_WIDTH
    o_gate = o_nkv + 6 * HEAD_DIM
    real(0, SSD_WIDTH)
    real(o_xbc, SSD_CONV_DIM)
    real(o_dt, SSD_HEADS, LANE)
    heads(o_sb, SB_HEADS)
    heads(o_sb + SB_WIDTH, SB_HEADS)
    heads(o_sb + 2 * SB_WIDTH, SB_HEADS)
    heads(o_nq, NSA_HEADS)
    heads(o_nkv, 6)
    real(o_gate, 3 * NSA_HEADS, LANE)
    src = np.asarray(src, np.int32)
    assert src.shape[0] == _IN_TOTAL
    return src


_IN_COLMAP = _inproj_colmap()


def _pad_cols(w, colmap):
    g = jnp.take(w, jnp.asarray(np.maximum(colmap, 0)), axis=-1)
    return jnp.where(jnp.asarray(colmap >= 0), g, 0.0)


def _pad_heads_vec(v, nheads):
    v = v.reshape(nheads, HEAD_DIM)
    v = jnp.pad(v, ((0, 0), (0, SLOT - HEAD_DIM)))
    return v.reshape(1, nheads * SLOT)


def _inproj_kernel(x_ref, nw_ref, w_ref, *out_refs):
    x = x_ref[...]
    h = _rms(x, nw_ref[...], D_MODEL).astype(BF16)
    off = 0
    for (name, width, dt), o_ref in zip(_IN_SEGS, out_refs):
        o_ref[...] = jnp.dot(h, w_ref[:, off:off + width], preferred_element_type=F32).astype(dt)
        off += width


def _inproj(xt, norm_w, w_pad):
    T = xt.shape[0]
    tm = 256
    out_shape = [jax.ShapeDtypeStruct((T, w), dt) for (_, w, dt) in _IN_SEGS]
    out_specs = [pl.BlockSpec((tm, w), lambda i: (i, 0)) for (_, w, _) in _IN_SEGS]
    return pl.pallas_call(
        _inproj_kernel,
        grid=(T // tm,),
        in_specs=[pl.BlockSpec((tm, D_MODEL), lambda i: (i, 0)),
                  pl.BlockSpec((1, D_MODEL), lambda i: (0, 0)),
                  pl.BlockSpec((D_MODEL, _IN_TOTAL), lambda i: (0, 0))],
        out_specs=out_specs,
        out_shape=out_shape,
        compiler_params=_cparams("parallel"),
    )(xt, norm_w.reshape(1, D_MODEL), w_pad)


def _ssd_kernel(z_ref, xbc_ref, dt_ref, cw_ref, cb_ref, dtb_ref, alog_ref, dsk_ref, nw_ref,
                o_ref, xbuf, state):
    L = SSD_CHUNK
    c = pl.program_id(1)

    @pl.when(c == 0)
    def _():
        xbuf[0:SUBLANE, :] = jnp.zeros((SUBLANE, SSD_CONV_DIM), F32)
        state[...] = jnp.zeros_like(state)

    xcur = xbc_ref[...]
    xbuf[SUBLANE:SUBLANE + L, :] = xcur
    acc = jnp.broadcast_to(cb_ref[...], (L, SSD_CONV_DIM))
    for k in range(SSD_CONV):
        s0 = SUBLANE - (SSD_CONV - 1) + k
        acc = acc + xbuf[s0:s0 + L, :] * cw_ref[k:k + 1, :]
    xbuf[0:SUBLANE, :] = xcur[L - SUBLANE:L, :]
    xc = acc * _sigmoid(acc)

    xs = xc[:, 0:SSD_WIDTH]
    dt = _softplus(dt_ref[...] + dtb_ref[...])
    a_dec = dt * (-jnp.exp(alog_ref[...]))

    row = _iota((L, L), 0)
    col = _iota((L, L), 1)
    tril = row >= col
    tri = tril.astype(BF16)
    cs = _dot_split_rhs(tri, a_dec)
    e64 = (_iota((LANE, SSD_WIDTH), 0) == _iota((LANE, SSD_WIDTH), 1) // HEAD_DIM).astype(BF16)
    e128 = (_iota((LANE, SSD_HEADS * LANE), 0) == _iota((LANE, SSD_HEADS * LANE), 1) // LANE).astype(BF16)
    dt_e = _dot_split_lhs(dt, e64)
    cs_e = _dot_split_lhs(cs, e64)
    cs_e2 = _dot_split_lhs(cs, e128)
    cs_t = cs.T
    cs_last = cs_e[L - 1:L, :]
    x_dt = xs * dt_e
    x_dec = x_dt * jnp.exp(cs_last - cs_e)
    in_dec = jnp.exp(cs_e)
    chunk_dec = jnp.exp(cs_last)
    lane_lo = _iota((L, LANE), 1) < HEAD_DIM

    ys = []
    for g in range(SSD_GROUPS):
        b_g = xc[:, SSD_WIDTH + g * SSD_STATE:SSD_WIDTH + (g + 1) * SSD_STATE]
        c_g = xc[:, SSD_WIDTH + (SSD_GROUPS + g) * SSD_STATE:SSD_WIDTH + (SSD_GROUPS + g + 1) * SSD_STATE]
        cb = _dot_nt(c_g, b_g)
        b_gt = b_g.T
        for m in range(2 * g, 2 * g + 2):
            lanes = slice(m * LANE, (m + 1) * LANE)
            xp = x_dt[:, lanes]
            yd = []
            for h in (2 * m, 2 * m + 1):
                diff = cs_e2[:, h * LANE:(h + 1) * LANE] - cs_t[h:h + 1, :]
                lm = jnp.exp(jnp.where(tril, diff, -jnp.inf))
                yd.append(_dot(lm * cb, xp))
            y_diag = jnp.where(lane_lo, yd[0], yd[1])
            s_prev = state[m]
            y_off = _dot(c_g, s_prev) * in_dec[:, lanes]
            state[m] = s_prev * chunk_dec[:, lanes] + _dot(b_gt, x_dec[:, lanes])
            ys.append(y_diag + y_off + xs[:, lanes] * dsk_ref[:, lanes])
    y = jnp.concatenate(ys, axis=1)
    zt = z_ref[...]
    o_ref[...] = _rms(y * (zt * _sigmoid(zt)), nw_ref[...], SSD_WIDTH)


def _ssd(z, xbc, dt, conv_w, conv_b, dt_bias, a_log, d_skip, norm_w, B, S):
    T = B * S
    L = SSD_CHUNK
    nc = S // L
    pad8 = lambda v: jnp.pad(v, (0, LANE - SSD_HEADS)).reshape(1, LANE)
    row = lambda b, c: (b * nc + c, 0)
    fixed = lambda b, c: (0, 0)
    return pl.pallas_call(
        _ssd_kernel,
        grid=(B, nc),
        in_specs=[pl.BlockSpec((L, SSD_WIDTH), row),
                  pl.BlockSpec((L, SSD_CONV_DIM), row),
                  pl.BlockSpec((L, LANE), row),
                  pl.BlockSpec((SSD_CONV, SSD_CONV_DIM), fixed),
                  pl.BlockSpec((1, SSD_CONV_DIM), fixed),
                  pl.BlockSpec((1, LANE), fixed),
                  pl.BlockSpec((1, LANE), fixed),
                  pl.BlockSpec((1, SSD_WIDTH), fixed),
                  pl.BlockSpec((1, SSD_WIDTH), fixed)],
        out_specs=pl.BlockSpec((L, SSD_WIDTH), row),
        out_shape=jax.ShapeDtypeStruct((T, SSD_WIDTH), F32),
        scratch_shapes=[pltpu.VMEM((SUBLANE + L, SSD_CONV_DIM), F32),
                        pltpu.VMEM((SSD_HEADS // 2, SSD_STATE, LANE), F32)],
        compiler_params=_cparams("parallel", "arbitrary"),
    )(z, xbc, dt, conv_w, conv_b.reshape(1, -1), pad8(dt_bias), pad8(a_log),
      jnp.repeat(d_skip, HEAD_DIM).reshape(1, SSD_WIDTH), norm_w.reshape(1, SSD_WIDTH))


SB_LOG_CUTOFF = -104.0


def _sb_kernel(q_ref, k_ref, v_ref, o_ref, cf_ref):
    tq = Q_TILE
    H = SB_HEADS
    qi = pl.program_id(1)
    row = _iota((tq, tq), 0)
    col = _iota((tq, tq), 1)
    strict = col < row
    r2 = _iota((tq, 2 * tq), 0)
    c2 = _iota((tq, 2 * tq), 1)
    umat = ((r2 > c2) | (c2 >= tq)).astype(BF16)
    scale = HEAD_DIM ** -0.5
    o_ref[...] = jnp.zeros_like(o_ref)
    cf_ref[...] = jnp.zeros_like(cf_ref)

    def cond(c):
        i, live = c
        return jnp.logical_and(i <= qi, live > 0)

    def body(c):
        i, _ = c
        ks = pl.multiple_of((qi - i) * tq, tq)
        mask = jnp.logical_or(i > 0, strict)[None]
        head_lanes = [slice(h * SLOT, (h + 1) * SLOT) for h in range(H)]
        z = jnp.concatenate([_dot_nt(q_ref[:, ln], k_ref[pl.ds(ks, tq), ln]) for ln in head_lanes], axis=0)
        z = (z * scale).reshape(H, tq, tq)
        l1p = jnp.log1p(jnp.exp(-jnp.abs(z)))
        log_beta = jnp.where(mask, jnp.minimum(z, 0.0) - l1p, -jnp.inf).reshape(H * tq, tq)
        log_stay = jnp.where(mask, -jnp.maximum(z, 0.0) - l1p, 0.0).reshape(H * tq, tq)
        r = _dot_split_lhs(log_stay, umat, 2)
        cf = cf_ref[...]
        w = jnp.exp(log_beta + r[:, 0:tq] + cf)
        for h, ln in enumerate(head_lanes):
            o_ref[:, ln] += _dot(w[h * tq:(h + 1) * tq, :], v_ref[pl.ds(ks, tq), ln])
        cf = cf + r[:, tq:2 * tq]
        cf_ref[...] = cf
        live = (jnp.max(cf) > SB_LOG_CUTOFF).astype(jnp.int32)
        return i + 1, live

    lax.while_loop(cond, body, (jnp.int32(0), jnp.int32(1)))


def _sb(sbq, sbk, sbv, B, S):
    T = B * S
    nq = S // Q_TILE
    W = SB_HEADS * SLOT
    return pl.pallas_call(
        _sb_kernel,
        grid=(B, nq),
        in_specs=[pl.BlockSpec((Q_TILE, W), lambda b, i: (b * nq + i, 0)),
                  pl.BlockSpec((S, W), lambda b, i: (b, 0)),
                  pl.BlockSpec((S, W), lambda b, i: (b, 0))],
        out_specs=pl.BlockSpec((Q_TILE, W), lambda b, i: (b * nq + i, 0)),
        out_shape=jax.ShapeDtypeStruct((T, W), F32),
        scratch_shapes=[pltpu.VMEM((SB_HEADS * Q_TILE, SLOT), F32)],
        compiler_params=_cparams("parallel", "arbitrary"),
    )(sbq, sbk, sbv)


def _rope_tables(S):
    pos = jnp.arange(S, dtype=F32)
    inv_freq = ROPE_THETA ** (-jnp.arange(0, ROT_DIM, 2, dtype=F32) / ROT_DIM)
    ang = pos[:, None] * inv_freq[None, :]
    cos, sin = jnp.cos(ang), jnp.sin(ang)
    half = ROT_DIM // 2
    ones = jnp.ones((S, SLOT - ROT_DIM), F32)
    zeros_h = jnp.zeros((S, half), F32)
    zeros_r = jnp.zeros((S, SLOT - ROT_DIM), F32)
    cosf = jnp.concatenate([cos, cos, ones], axis=1)
    sin_a = jnp.concatenate([-sin, zeros_h, zeros_r], axis=1)
    sin_b = jnp.concatenate([zeros_h, sin, zeros_r], axis=1)
    return cosf, sin_a, sin_b


def _nsa_prep_kernel(q_ref, kv_ref, cos_ref, sa_ref, sb_ref, qw_ref, kw_ref,
                     qo_ref, kc_ref, vc_ref, ksw_ref):
    cos, sa, sb = cos_ref[...], sa_ref[...], sb_ref[...]
    half = ROT_DIM // 2

    def rope(x):
        return x * cos + pltpu.roll(x, SLOT - half, 1) * sa + pltpu.roll(x, half, 1) * sb

    qw = qw_ref[...]
    for h in range(NSA_HEADS):
        lanes = slice(h * SLOT, (h + 1) * SLOT)
        qo_ref[:, lanes] = rope(_rms(q_ref[:, lanes], qw, HEAD_DIM)).astype(BF16)
    kc_ref[...] = rope(kv_ref[:, 0:SLOT])
    vc_ref[...] = kv_ref[:, SLOT:2 * SLOT]
    ksw_ref[:, 0:SLOT] = rope(_rms(kv_ref[:, 2 * SLOT:3 * SLOT], kw_ref[1:2, :], HEAD_DIM)).astype(BF16)
    ksw_ref[:, SLOT:2 * SLOT] = kv_ref[:, 3 * SLOT:4 * SLOT].astype(BF16)
    ksw_ref[:, 2 * SLOT:3 * SLOT] = rope(_rms(kv_ref[:, 4 * SLOT:5 * SLOT], kw_ref[2:3, :], HEAD_DIM)).astype(BF16)
    ksw_ref[:, 3 * SLOT:4 * SLOT] = kv_ref[:, 5 * SLOT:6 * SLOT].astype(BF16)


def _nsa_prep(nq, nkv, ropes, q_norm_w, k_norm_w, B, S):
    T = B * S
    ts = 256
    nt = S // ts
    cosf, sin_a, sin_b = ropes
    pad = lambda v: jnp.pad(v, ((0, 0), (0, SLOT - HEAD_DIM)))
    row = lambda b, i: (b * nt + i, 0)
    pos = lambda b, i: (i, 0)
    fixed = lambda b, i: (0, 0)
    return pl.pallas_call(
        _nsa_prep_kernel,
        grid=(B, nt),
        in_specs=[pl.BlockSpec((ts, NSA_HEADS * SLOT), row),
                  pl.BlockSpec((ts, 6 * SLOT), row),
                  pl.BlockSpec((ts, SLOT), pos),
                  pl.BlockSpec((ts, SLOT), pos),
                  pl.BlockSpec((ts, SLOT), pos),
                  pl.BlockSpec((1, SLOT), fixed),
                  pl.BlockSpec((3, SLOT), fixed)],
        out_specs=[pl.BlockSpec((ts, NSA_HEADS * SLOT), row),
                   pl.BlockSpec((ts, SLOT), row),
                   pl.BlockSpec((ts, SLOT), row),
                   pl.BlockSpec((ts, 4 * SLOT), row)],
        out_shape=[jax.ShapeDtypeStruct((T, NSA_HEADS * SLOT), BF16),
                   jax.ShapeDtypeStruct((T, SLOT), F32),
                   jax.ShapeDtypeStruct((T, SLOT), F32),
                   jax.ShapeDtypeStruct((T, 4 * SLOT), BF16)],
        compiler_params=_cparams("parallel", "parallel"),
    )(nq, nkv, cosf, sin_a, sin_b, pad(q_norm_w.reshape(1, HEAD_DIM)), pad(k_norm_w))


def _nsa_cmp_kernel(kc_ref, vc_ref, pos_ref, w_ref, knw_ref, ko_ref, vo_ref):
    nseg = ko_ref.shape[0]
    half = CMP_LEN // 2
    for which, (src, dst) in enumerate(((kc_ref, ko_ref), (vc_ref, vo_ref))):
        f1 = jnp.zeros((nseg, SLOT), F32)
        f2 = jnp.zeros((nseg, SLOT), F32)
        for p in range(half):
            rows = src[pl.ds(p, nseg, stride=CMP_STRIDE), :]
            f1 = f1 + _dot(rows + pos_ref[which, p:p + 1, :], w_ref[which, p])
            f2 = f2 + _dot(rows + pos_ref[which, half + p:half + p + 1, :], w_ref[which, half + p])
        pre = f1 + pltpu.roll(f2, nseg - 1, 0)
        if which == 0:
            pre = _rms(pre, knw_ref[...], HEAD_DIM)
        dst[...] = pre


def _nsa_cmp(kc, vc, cmp_pos, cmp_w, k_norm_w0, B, S):
    nseg = S // CMP_STRIDE
    pos = jnp.pad(cmp_pos, ((0, 0), (0, 0), (0, SLOT - HEAD_DIM)))
    w = cmp_w.reshape(2, CMP_LEN, HEAD_DIM, HEAD_DIM)
    w = jnp.pad(w, ((0, 0), (0, 0), (0, SLOT - HEAD_DIM), (0, SLOT - HEAD_DIM))).astype(BF16)
    knw = jnp.pad(k_norm_w0.reshape(1, HEAD_DIM), ((0, 0), (0, SLOT - HEAD_DIM)))
    per_b = lambda b: (b, 0)
    return pl.pallas_call(
        _nsa_cmp_kernel,
        grid=(B,),
        in_specs=[pl.BlockSpec((S, SLOT), per_b),
                  pl.BlockSpec((S, SLOT), per_b),
                  pl.BlockSpec((2, CMP_LEN, SLOT), lambda b: (0, 0, 0)),
                  pl.BlockSpec((2, CMP_LEN, SLOT, SLOT), lambda b: (0, 0, 0, 0)),
                  pl.BlockSpec((1, SLOT), lambda b: (0, 0))],
        out_specs=[pl.BlockSpec((nseg, SLOT), per_b), pl.BlockSpec((nseg, SLOT), per_b)],
        out_shape=[jax.ShapeDtypeStruct((B * nseg, SLOT), F32),
                   jax.ShapeDtypeStruct((B * nseg, SLOT), F32)],
        compiler_params=_cparams("parallel"),
    )(kc, vc, pos, w, knw)


def _nsa_attn_kernel(q_ref, kc_ref, vc_ref, ksw_ref, g_ref, ow_ref, o_ref, *, S):
    tq = Q_TILE
    n_cmp_pad = S // CMP_STRIDE
    n_sel = S // SEL_BLOCK
    n_top = min(SEL_TOPK, n_sel)
    n_win = WINDOW + tq
    scale = HEAD_DIM ** -0.5
    qi = pl.program_id(1)
    t0 = qi * tq

    t_c = t0 + _iota((tq, n_cmp_pad), 0)
    c_c = _iota((tq, n_cmp_pad), 1)
    vis = (c_c * CMP_STRIDE + (CMP_LEN - 1)) <= t_c
    kc = kc_ref[...]
    vc = vc_ref[...]
    H = NSA_HEADS
    q4 = jnp.concatenate([q_ref[:, h * SLOT:(h + 1) * SLOT] for h in range(H)], axis=0)

    q4s = q4 * scale

    def attend(k, v, ok, nkeys):
        s = jnp.where(ok[None], _dot_nt(q4s, k).reshape(H, tq, nkeys), NEG_BIG)
        e = jnp.exp(s - jnp.max(s, axis=-1, keepdims=True)).reshape(H * tq, nkeys)
        return _dot(e, v) / jnp.sum(e, axis=-1, keepdims=True)

    s = _dot_nt(q4s, kc).reshape(H, tq, n_cmp_pad)
    p = jnp.where(vis[None], _softmax_rows(jnp.where(vis[None], s, NEG_BIG)), 0.0)
    o_cmp = _dot(p.reshape(H * tq, n_cmp_pad), vc)
    p_sum = jnp.sum(p, axis=0)

    r_o = _iota((n_cmp_pad, LANE), 0) * CMP_STRIDE
    j_o = _iota((n_cmp_pad, LANE), 1) * SEL_BLOCK
    overlap = ((r_o < j_o + SEL_BLOCK) & (r_o + CMP_LEN > j_o)
               & (_iota((n_cmp_pad, LANE), 0) < n_cmp_pad - 1)).astype(BF16)
    imp_t = _dot_split_lhs(p_sum, overlap).T
    j_t = _iota((LANE, tq), 0)
    t_t = t0 + _iota((LANE, tq), 1)
    blk = t_t // SEL_BLOCK
    forced = (j_t == 0) | (j_t == blk) | (j_t == blk - 1)
    causal = (j_t * SEL_BLOCK <= t_t) & (j_t < n_sel)
    val = jnp.where(causal, jnp.where(forced, FORCE_SCORE, imp_t), -jnp.inf)
    rank = jnp.zeros((LANE, tq), jnp.int32)
    for jp in range(n_sel):
        vj = val[jp:jp + 1, :]
        beats = (vj > val) | ((vj == val) & (j_t > jp))
        rank = rank + beats.astype(jnp.int32)
    sel_t = (causal & (rank < n_top)).astype(F32)
    sel = sel_t.T.astype(BF16)
    expand = (_iota((LANE, S), 0) == _iota((LANE, S), 1) // SEL_BLOCK).astype(BF16)
    sel_keys = jnp.dot(sel, expand, preferred_element_type=F32)

    t_s = t0 + _iota((tq, S), 0)
    k_s = _iota((tq, S), 1)
    ok_sel = (sel_keys > 0.5) & (k_s <= t_s)
    o_sel = attend(ksw_ref[:, 0:SLOT], ksw_ref[:, SLOT:2 * SLOT], ok_sel, S)

    start = pl.multiple_of(jnp.clip(t0 - WINDOW, 0, S - n_win), tq)
    k_win = ksw_ref[pl.ds(start, n_win), 2 * SLOT:3 * SLOT]
    v_win = ksw_ref[pl.ds(start, n_win), 3 * SLOT:4 * SLOT]
    rel = (t0 + _iota((tq, n_win), 0)) - (start + _iota((tq, n_win), 1))
    ok_win = (rel >= 0) & (rel < WINDOW)
    o_win = attend(k_win, v_win, ok_win, n_win)

    gate = _sigmoid(g_ref[...])
    n_g = 3 * NSA_HEADS
    e_g = (_iota((LANE, n_g * SLOT), 0) == _iota((LANE, n_g * SLOT), 1) // SLOT).astype(BF16)
    g_e = _dot_split_lhs(gate, e_g)
    outs = []
    for h in range(NSA_HEADS):
        g0 = g_e[:, (3 * h) * SLOT:(3 * h + 1) * SLOT]
        g1 = g_e[:, (3 * h + 1) * SLOT:(3 * h + 2) * SLOT]
        g2 = g_e[:, (3 * h + 2) * SLOT:(3 * h + 3) * SLOT]
        rows = slice(h * tq, (h + 1) * tq)
        outs.append(g0 * o_cmp[rows, :] + g1 * o_sel[rows, :] + g2 * o_win[rows, :])
    o = jnp.concatenate(outs, axis=1)
    o_ref[...] = _rms(o, ow_ref[...], NSA_WIDTH)


def _nsa_attn(q_r, k_cmp, v_cmp, ksw, gate, out_w, B, S):
    T = B * S
    nq = S // Q_TILE
    nseg = S // CMP_STRIDE
    row = lambda b, i: (b * nq + i, 0)
    per_b = lambda b, i: (b, 0)
    return pl.pallas_call(
        functools.partial(_nsa_attn_kernel, S=S),
        grid=(B, nq),
        in_specs=[pl.BlockSpec((Q_TILE, NSA_HEADS * SLOT), row),
                  pl.BlockSpec((nseg, SLOT), per_b),
                  pl.BlockSpec((nseg, SLOT), per_b),
                  pl.BlockSpec((S, 4 * SLOT), per_b),
                  pl.BlockSpec((Q_TILE, LANE), row),
                  pl.BlockSpec((1, NSA_HEADS * SLOT), lambda b, i: (0, 0))],
        out_specs=pl.BlockSpec((Q_TILE, NSA_HEADS * SLOT), row),
        out_shape=jax.ShapeDtypeStruct((T, NSA_HEADS * SLOT), F32),
        compiler_params=_cparams("parallel", "parallel"),
    )(q_r, k_cmp, v_cmp, ksw, gate, _pad_heads_vec(out_w, NSA_HEADS))


def _outproj_kernel(x_ref, yssd_ref, ysb_ref, ynsa_ref, sbw_ref, w_ref, o_ref):
    ysb = _rms(ysb_ref[...], sbw_ref[...], SB_WIDTH)
    w0 = SSD_WIDTH
    w1 = w0 + SB_HEADS * SLOT
    w2 = w1 + NSA_HEADS * SLOT
    acc = x_ref[...]
    acc = acc + jnp.dot(yssd_ref[...].astype(BF16), w_ref[0:w0, :], preferred_element_type=F32)
    acc = acc + jnp.dot(ysb.astype(BF16), w_ref[w0:w1, :], preferred_element_type=F32)
    acc = acc + jnp.dot(ynsa_ref[...].astype(BF16), w_ref[w1:w2, :], preferred_element_type=F32)
    o_ref[...] = acc


def _outproj_rowmap():
    src = list(range(SSD_WIDTH))
    for base, nheads in ((SSD_WIDTH, SB_HEADS), (SSD_WIDTH + SB_WIDTH, NSA_HEADS)):
        for h in range(nheads):
            src.extend(range(base + h * HEAD_DIM, base + (h + 1) * HEAD_DIM))
            src.extend([-1] * (SLOT - HEAD_DIM))
    return np.asarray(src, np.int32)


_OUT_ROWMAP = _outproj_rowmap()


def _outproj(xt, y_ssd, y_sb, y_nsa, sb_out_w, w_pad):
    T = xt.shape[0]
    tm = 256
    kdim = _OUT_ROWMAP.shape[0]
    row = lambda i: (i, 0)
    fixed = lambda i: (0, 0)
    return pl.pallas_call(
        _outproj_kernel,
        grid=(T // tm,),
        in_specs=[pl.BlockSpec((tm, D_MODEL), row),
                  pl.BlockSpec((tm, SSD_WIDTH), row),
                  pl.BlockSpec((tm, SB_HEADS * SLOT), row),
                  pl.BlockSpec((tm, NSA_HEADS * SLOT), row),
                  pl.BlockSpec((1, SB_HEADS * SLOT), fixed),
                  pl.BlockSpec((kdim, D_MODEL), fixed)],
        out_specs=pl.BlockSpec((tm, D_MODEL), row),
        out_shape=jax.ShapeDtypeStruct((T, D_MODEL), F32),
        compiler_params=_cparams("parallel"),
    )(xt, y_ssd, y_sb, y_nsa, _pad_heads_vec(sb_out_w, SB_HEADS), w_pad)


def _cmp_exchange(x, y):
    if y is None:
        return x, None
    if x is None:
        return y, None
    return jnp.maximum(x, y), jnp.minimum(x, y)


def _bitonic_sort_desc(u):
    u = list(u)
    n = len(u)
    k = 2
    while k <= n:
        j = k // 2
        while j >= 1:
            for i in range(n):
                l = i ^ j
                if l > i:
                    hi, lo = _cmp_exchange(u[i], u[l])
                    u[i], u[l] = (hi, lo) if (i & k) == 0 else (lo, hi)
            j //= 2
        k *= 2
    return u


def _merge_top(a, b):
    n = len(a)
    c = [_cmp_exchange(a[i], b[n - 1 - i])[0] for i in range(n)]
    d = n // 2
    while d >= 1:
        for i in range(n):
            if (i & d) == 0:
                c[i], c[i + d] = _cmp_exchange(c[i], c[i + d])
        d //= 2
    return c


def _top16_replicated(s):
    groups = s.shape[0] // SUBLANE
    u = _bitonic_sort_desc([s[g * SUBLANE:(g + 1) * SUBLANE, :] for g in range(groups)])
    shift = SUBLANE // 2
    while shift >= 1:
        u = _merge_top(u, [pltpu.roll(x, shift, 0) for x in u])
        shift //= 2
    return u


def _bf16_pair_words(x):
    bits = lax.bitcast_convert_type(x.astype(BF16).astype(F32), jnp.uint32)
    return bits | (bits >> 16)


def _peer_select_kernel(x_ref, nw_ref, wq_ref, sk_ref,
                        h_ref, r_ref, e1_ref, rk_ref, e2_ref):
    K = PEER_TOPK
    half = PEER_QDIM // 2
    nk = N_KEYS
    h = _rms(x_ref[...], nw_ref[...], D_MODEL)
    h_ref[...] = h.T.astype(BF16)
    q_t = _dot_nt(wq_ref[...], h)
    groups = nk // SUBLANE
    tokens = q_t.shape[1]
    for hd, t0 in [(hd, t0) for hd in range(PEER_HEADS) for t0 in range(0, tokens, LANE)]:
        lanes = slice(t0, t0 + LANE)
        r0 = hd * 2 * half
        if t0 == 0:
            s1_all = _dot(sk_ref[0], q_t[r0:r0 + half, :])
            s2_all = _dot(sk_ref[1], q_t[r0 + half:r0 + 2 * half, :])
        s1, s2 = s1_all[:, lanes], s2_all[:, lanes]
        sv1 = _top16_replicated(s1)
        sv2 = _top16_replicated(s2)
        lists = []
        for a in range(K):
            row = [sv1[a] + sv2[b] for b in range(K // (a + 1))]
            lists.append(row + [None] * (K - len(row)))
        while len(lists) > 1:
            nxt = [_merge_top(lists[i], lists[i + 1]) for i in range(0, len(lists) - 1, 2)]
            if len(lists) % 2:
                nxt.append(lists[-1])
            lists = nxt
        cv = lists[0]
        tau = cv[K - 1]
        zsum = None
        for k in range(K):
            e = jnp.exp(cv[k] - cv[0])
            zsum = e if zsum is None else zsum + e
        inv_z = 1.0 / zsum
        r_parts, rk_parts, e1_parts, e2_parts = [], [], [], []
        for g in range(groups):
            rows = slice(g * SUBLANE, (g + 1) * SUBLANE)
            a1, a2 = s1[rows, :], s2[rows, :]
            cnt = jnp.full(a1.shape, -1.0, F32)
            rk = jnp.zeros(a2.shape, F32)
            for b in range(K):
                cnt = cnt + jnp.where(a1 + sv2[b] >= tau, 1.0, 0.0)
                rk = rk + jnp.where(sv2[b] > a2, 1.0, 0.0)
            r_parts.append(cnt)
            rk_parts.append(rk)
            e1_parts.append(jnp.exp(a1 - sv1[0]))
            e2_parts.append(jnp.exp(a2 - sv2[0]) * inv_z)
        r_ref[hd, :, lanes] = _bf16_pair_words(jnp.concatenate(r_parts, axis=0))
        e1_ref[hd, :, lanes] = _bf16_pair_words(jnp.concatenate(e1_parts, axis=0))
        rk_ref[hd, :, lanes] = jnp.concatenate(rk_parts, axis=0).astype(BF16)
        e2_ref[hd, :, lanes] = jnp.concatenate(e2_parts, axis=0).astype(BF16)


BF16_ROWS = 2 * SUBLANE
GATE_LANES = 2 * LANE


def _peer_dense_kernel(h_ref, x_ref, u_ref, vt_ref, r_ref, e1_ref, rk_ref, e2_ref,
                       o_ref, acc_ref, p_ref):
    c = pl.program_id(1)
    nk = N_KEYS
    tb = h_ref.shape[1]

    @pl.when(c == 0)
    def _():
        acc_ref[...] = jnp.zeros_like(acc_ref)

    h_t = jnp.dot(u_ref[...], h_ref[...], preferred_element_type=F32)
    row0 = pl.multiple_of(c * SUBLANE, SUBLANE)
    r_rows = [r_ref[hd, pl.ds(row0, SUBLANE), :] for hd in range(PEER_HEADS)]
    e1_rows = [e1_ref[hd, pl.ds(row0, SUBLANE), :] for hd in range(PEER_HEADS)]
    tw = GATE_LANES
    for t0 in range(0, tb, tw):
        lanes = slice(t0, t0 + tw)
        for il in range(SUBLANE):
            a = h_t[il * nk:(il + 1) * nk, lanes]
            act = (0.5 * a * (1.0 + lax.erf(a * (2.0 ** -0.5)))).astype(BF16)
            rb = [pltpu.bitcast(jnp.broadcast_to(r_rows[hd][il:il + 1, lanes], (SUBLANE, tw)), BF16)
                  for hd in range(PEER_HEADS)]
            eb = [pltpu.bitcast(jnp.broadcast_to(e1_rows[hd][il:il + 1, lanes], (SUBLANE, tw)), BF16)
                  for hd in range(PEER_HEADS)]
            for j in range(nk // BF16_ROWS):
                rows = pl.ds(j * BF16_ROWS, BF16_ROWS)
                w = None
                for hd in range(PEER_HEADS):
                    t = jnp.where(rk_ref[hd, rows, lanes] <= rb[hd], e2_ref[hd, rows, lanes],
                                  jnp.zeros((), BF16)) * eb[hd]
                    w = t if w is None else w + t
                p_ref[pl.ds(il * nk + j * BF16_ROWS, BF16_ROWS), lanes] = (
                    w * act[j * BF16_ROWS:(j + 1) * BF16_ROWS, :])
    acc_ref[...] += jnp.dot(vt_ref[...], p_ref[...], preferred_element_type=F32)

    @pl.when(c == pl.num_programs(1) - 1)
    def _():
        o_ref[...] = x_ref[...] + acc_ref[...].T


def _peer(x1, norm_w, wq_t, sub_keys, u_bf, vt_bf):
    T = x1.shape[0]
    H, nk = PEER_HEADS, N_KEYS
    tb_s = 256
    row = lambda i: (i, 0)
    fixed2 = lambda i: (0, 0)
    tok3 = lambda i: (0, 0, i)
    head_f32 = jax.ShapeDtypeStruct((H, nk, T), jnp.uint32)
    head_bf16 = jax.ShapeDtypeStruct((H, nk, T), BF16)
    h_bf, r_cnt, e1, rk2, e2 = pl.pallas_call(
        _peer_select_kernel,
        grid=(T // tb_s,),
        in_specs=[pl.BlockSpec((tb_s, D_MODEL), row),
                  pl.BlockSpec((1, D_MODEL), fixed2),
                  pl.BlockSpec((H * PEER_QDIM, D_MODEL), fixed2),
                  pl.BlockSpec((2, nk, PEER_QDIM // 2), lambda i: (0, 0, 0))],
        out_specs=[pl.BlockSpec((D_MODEL, tb_s), lambda i: (0, i))] + [pl.BlockSpec((H, nk, tb_s), tok3)] * 4,
        out_shape=[jax.ShapeDtypeStruct((D_MODEL, T), BF16), head_f32, head_f32, head_bf16, head_bf16],
        compiler_params=_cparams("parallel"),
    )(x1, norm_w.reshape(1, D_MODEL), wq_t, sub_keys)

    tb = 1024
    chunk = SUBLANE
    once = pl.Buffered(1)
    gate_spec = pl.BlockSpec((H, nk, tb), lambda i, c: (0, 0, i), pipeline_mode=once)
    return pl.pallas_call(
        _peer_dense_kernel,
        grid=(T // tb, nk // chunk),
        in_specs=[pl.BlockSpec((D_MODEL, tb), lambda i, c: (0, i), pipeline_mode=once),
                  pl.BlockSpec((tb, D_MODEL), lambda i, c: (i, 0), pipeline_mode=once),
                  pl.BlockSpec((chunk * nk, D_MODEL), lambda i, c: (c, 0)),
                  pl.BlockSpec((D_MODEL, chunk * nk), lambda i, c: (0, c))]
                 + [gate_spec] * 4,
        out_specs=pl.BlockSpec((tb, D_MODEL), lambda i, c: (i, 0)),
        out_shape=jax.ShapeDtypeStruct((T, D_MODEL), F32),
        scratch_shapes=[pltpu.VMEM((D_MODEL, tb), F32),
                        pltpu.VMEM((chunk * nk, tb), BF16)],
        compiler_params=_cparams("parallel", "arbitrary"),
    )(h_bf, x1, u_bf, vt_bf, r_cnt, e1, rk2, e2)


def kernel(x, norm1_w, w_in, conv_w, conv_b, dt_bias, a_log, d_skip, ssd_norm_w, sb_out_w,
           nsa_q_norm_w, nsa_k_norm_w, nsa_cmp_pos, nsa_cmp_w, nsa_out_w, w_out, norm2_w,
           peer_w_q, peer_sub_keys, peer_u, peer_v):
    B, S, _ = x.shape
    depth = w_in.shape[0]
    assert S == 2048, "NSA tiling is laid out for 2048-token sequences"
    T = B * S
    ropes = _rope_tables(S)
    xt = x.reshape(T, D_MODEL)
    for l in range(depth):
        w_in_pad = _pad_cols(w_in[l], _IN_COLMAP).astype(BF16)
        z, xbc, dt, sbq, sbk, sbv, nq, nkv, gate = _inproj(xt, norm1_w[l], w_in_pad)
        y_ssd = _ssd(z, xbc, dt, conv_w[l], conv_b[l], dt_bias[l], a_log[l], d_skip[l], ssd_norm_w[l], B, S)
        y_sb = _sb(sbq, sbk, sbv, B, S)
        q_r, kc, vc, ksw = _nsa_prep(nq, nkv, ropes, nsa_q_norm_w[l], nsa_k_norm_w[l], B, S)
        k_cmp, v_cmp = _nsa_cmp(kc, vc, nsa_cmp_pos[l], nsa_cmp_w[l], nsa_k_norm_w[l, 0], B, S)
        y_nsa = _nsa_attn(q_r, k_cmp, v_cmp, ksw, gate, nsa_out_w[l], B, S)
        w_out_pad = _pad_cols(w_out[l].T, _OUT_ROWMAP).T.astype(BF16)
        x1 = _outproj(xt, y_ssd, y_sb, y_nsa, sb_out_w[l], w_out_pad)
        xt = _peer(x1, norm2_w[l], peer_w_q[l].T.astype(BF16), peer_sub_keys[l],
                   peer_u[l].astype(BF16), peer_v[l].T.astype(BF16))
    return xt.reshape(B, S, D_MODEL)
```

```python
import functools
import math

import numpy as np
import jax
import jax.numpy as jnp
from jax import lax
from jax.experimental import pallas as pl
from jax.experimental.pallas import tpu as pltpu

F32 = jnp.float32
BF16 = jnp.bfloat16

D_MODEL = 1024
HEAD_DIM = 64
SSD_WIDTH = 512
SSD_HEADS = 8
SSD_GROUPS = 2
SSD_STATE = 128
SSD_CONV = 4
SSD_CHUNK = 128
SSD_CONV_DIM = SSD_WIDTH + 2 * SSD_GROUPS * SSD_STATE
SB_WIDTH = 256
SB_HEADS = 4
NSA_WIDTH = 256
NSA_HEADS = 4
CMP_LEN = 32
CMP_STRIDE = 16
SEL_BLOCK = 64
SEL_TOPK = 16
WINDOW = 512
FORCE_SCORE = 1e9
ROT_DIM = 16
ROPE_THETA = 500000.0
PEER_HEADS = 8
N_KEYS = 128
PEER_TOPK = 16
PEER_QDIM = 256
NORM_EPS = 1e-6
NEG_BIG = -1e30

LANE = 128
SUBLANE = 8
V7X_VMEM_BYTES = 64 * 1024 * 1024
VMEM_LIMIT = V7X_VMEM_BYTES - 12 * 1024 * 1024

SLOT = LANE

Q_TILE = 128
ROW_TILE = 512
SELECT_TILE = 256
DENSE_TILE = 1024


def _cparams(*sem, flags=None):
    return pltpu.CompilerParams(dimension_semantics=sem, vmem_limit_bytes=VMEM_LIMIT, flags=flags)


def _iota(shape, dim):
    return lax.broadcasted_iota(jnp.int32, shape, dim)


def _dot(a, b):
    return jnp.dot(a.astype(BF16), b.astype(BF16), preferred_element_type=F32)


def _dot_nt(a, b):
    return lax.dot_general(a.astype(BF16), b.astype(BF16), (((1,), (1,)), ((), ())),
                           preferred_element_type=F32)


def _split(a, n):
    parts, r = [], a
    for i in range(n):
        p = r.astype(BF16)
        parts.append(p)
        if i + 1 < n:
            r = r - p.astype(F32)
    return parts


def _dot_split_lhs(a, b_exact, n=3):
    acc = None
    for p in _split(a, n):
        t = jnp.dot(p, b_exact, preferred_element_type=F32)
        acc = t if acc is None else acc + t
    return acc


def _dot_split_rhs(a_exact, b, n=3):
    acc = None
    for p in _split(b, n):
        t = jnp.dot(a_exact, p, preferred_element_type=F32)
        acc = t if acc is None else acc + t
    return acc


def _sigmoid(x):
    return 1.0 / (1.0 + jnp.exp(-x))


def _softplus(x):
    return jnp.maximum(x, 0.0) + jnp.log1p(jnp.exp(-jnp.abs(x)))


def _rms(x, w, n):
    ms = jnp.sum(x * x, axis=-1, keepdims=True) * (1.0 / n)
    return x * lax.rsqrt(ms + NORM_EPS) * w


def _softmax_rows(s):
    m = jnp.max(s, axis=-1, keepdims=True)
    e = jnp.exp(s - m)
    return e / jnp.sum(e, axis=-1, keepdims=True)


_IN_SEGS = (("z", 512, F32), ("xbc", 1024, F32), ("dt", 128, F32),
            ("sbq", 512, BF16), ("sbk", 512, BF16), ("sbv", 512, BF16),
            ("nq", 512, F32), ("nkv", 768, F32), ("gate", 128, F32))
_IN_TOTAL = sum(s[1] for s in _IN_SEGS)


def _inproj_colmap():
    src = []

    def real(a, n, pad_to=None):
        src.extend(range(a, a + n))
        if pad_to:
            src.extend([-1] * (pad_to - n))

    def heads(a, nheads):
        for h in range(nheads):
            src.extend(range(a + h * HEAD_DIM, a + (h + 1) * HEAD_DIM))
            src.extend([-1] * (SLOT - HEAD_DIM))

    o_xbc = SSD_WIDTH
    o_dt = o_xbc + SSD_CONV_DIM
    o_sb = o_dt + SSD_HEADS
    o_nq = o_sb + 3 * SB_WIDTH
    o_nkv = o_nq + NSA_WIDTH
    o_gate = o_nkv + 6 * HEAD_DIM
    real(0, SSD_WIDTH)
    real(o_xbc, SSD_CONV_DIM)
    real(o_dt, SSD_HEADS, LANE)
    heads(o_sb, SB_HEADS)
    heads(o_sb + SB_WIDTH, SB_HEADS)
    heads(o_sb + 2 * SB_WIDTH, SB_HEADS)
    heads(o_nq, NSA_HEADS)
    heads(o_nkv, 6)
    real(o_gate, 3 * NSA_HEADS, LANE)
    src = np.asarray(src, np.int32)
    assert src.shape[0] == _IN_TOTAL
    return src


_IN_COLMAP = _inproj_colmap()


def _pad_cols(w, colmap):
    g = jnp.take(w, jnp.asarray(np.maximum(colmap, 0)), axis=-1)
    return jnp.where(jnp.asarray(colmap >= 0), g, 0.0)


def _pad_heads_vec(v, nheads):
    v = v.reshape(nheads, HEAD_DIM)
    v = jnp.pad(v, ((0, 0), (0, SLOT - HEAD_DIM)))
    return v.reshape(1, nheads * SLOT)


def _inproj_kernel(x_ref, nw_ref, w_ref, *out_refs):
    x = x_ref[...]
    h = _rms(x, nw_ref[...], D_MODEL).astype(BF16)
    off = 0
    for (name, width, dt), o_ref in zip(_IN_SEGS, out_refs):
        o_ref[...] = jnp.dot(h, w_ref[:, off:off + width], preferred_element_type=F32).astype(dt)
        off += width


def _inproj(xt, norm_w, w_pad):
    T = xt.shape[0]
    tm = ROW_TILE
    out_shape = [jax.ShapeDtypeStruct((T, w), dt) for (_, w, dt) in _IN_SEGS]
    out_specs = [pl.BlockSpec((tm, w), lambda i: (i, 0)) for (_, w, _) in _IN_SEGS]
    return pl.pallas_call(
        _inproj_kernel,
        grid=(T // tm,),
        in_specs=[pl.BlockSpec((tm, D_MODEL), lambda i: (i, 0)),
                  pl.BlockSpec((1, D_MODEL), lambda i: (0, 0)),
                  pl.BlockSpec((D_MODEL, _IN_TOTAL), lambda i: (0, 0))],
        out_specs=out_specs,
        out_shape=out_shape,
        compiler_params=_cparams("parallel"),
    )(xt, norm_w.reshape(1, D_MODEL), w_pad)


def _ssd_kernel(z_ref, xbc_ref, dt_ref, cw_ref, cb_ref, dtb_ref, alog_ref, dsk_ref, nw_ref,
                o_ref, xbuf, state):
    L = SSD_CHUNK
    c = pl.program_id(1)

    @pl.when(c == 0)
    def _():
        xbuf[0:SUBLANE, :] = jnp.zeros((SUBLANE, SSD_CONV_DIM), F32)
        state[...] = jnp.zeros_like(state)

    xcur = xbc_ref[...]
    xbuf[SUBLANE:SUBLANE + L, :] = xcur
    acc = jnp.broadcast_to(cb_ref[...], (L, SSD_CONV_DIM))
    for k in range(SSD_CONV):
        s0 = SUBLANE - (SSD_CONV - 1) + k
        acc = acc + xbuf[s0:s0 + L, :] * cw_ref[k:k + 1, :]
    xbuf[0:SUBLANE, :] = xcur[L - SUBLANE:L, :]
    xc = acc * _sigmoid(acc)

    xs = xc[:, 0:SSD_WIDTH]
    dt = _softplus(dt_ref[...] + dtb_ref[...])
    a_dec = dt * (-jnp.exp(alog_ref[...]))

    row = _iota((L, L), 0)
    col = _iota((L, L), 1)
    tril = row >= col
    tri = tril.astype(BF16)
    cs = _dot_split_rhs(tri, a_dec)
    e64 = (_iota((LANE, SSD_WIDTH), 0) == _iota((LANE, SSD_WIDTH), 1) // HEAD_DIM).astype(BF16)
    e128 = (_iota((LANE, SSD_HEADS * LANE), 0) == _iota((LANE, SSD_HEADS * LANE), 1) // LANE).astype(BF16)
    dt_e = _dot_split_lhs(dt, e64)
    cs_e = _dot_split_lhs(cs, e64)
    cs_e2 = _dot_split_lhs(cs, e128)
    cs_t = cs.T
    cs_last = cs_e[L - 1:L, :]
    x_dt = xs * dt_e
    x_dec = x_dt * jnp.exp(cs_last - cs_e)
    in_dec = jnp.exp(cs_e)
    chunk_dec = jnp.exp(cs_last)
    lane_lo = _iota((L, LANE), 1) < HEAD_DIM

    ys = []
    for g in range(SSD_GROUPS):
        b_g = xc[:, SSD_WIDTH + g * SSD_STATE:SSD_WIDTH + (g + 1) * SSD_STATE]
        c_g = xc[:, SSD_WIDTH + (SSD_GROUPS + g) * SSD_STATE:SSD_WIDTH + (SSD_GROUPS + g + 1) * SSD_STATE]
        cb = _dot_nt(c_g, b_g)
        b_gt = b_g.T
        for m in range(2 * g, 2 * g + 2):
            lanes = slice(m * LANE, (m + 1) * LANE)
            xp = x_dt[:, lanes]
            yd = []
            for h in (2 * m, 2 * m + 1):
                diff = cs_e2[:, h * LANE:(h + 1) * LANE] - cs_t[h:h + 1, :]
                lm = jnp.exp(jnp.where(tril, diff, -jnp.inf))
                yd.append(_dot(lm * cb, xp))
            y_diag = jnp.where(lane_lo, yd[0], yd[1])
            s_prev = state[m]
            y_off = _dot(c_g, s_prev) * in_dec[:, lanes]
            state[m] = s_prev * chunk_dec[:, lanes] + _dot(b_gt, x_dec[:, lanes])
            ys.append(y_diag + y_off + xs[:, lanes] * dsk_ref[:, lanes])
    y = jnp.concatenate(ys, axis=1)
    zt = z_ref[...]
    o_ref[...] = _rms(y * (zt * _sigmoid(zt)), nw_ref[...], SSD_WIDTH)


def _ssd(z, xbc, dt, conv_w, conv_b, dt_bias, a_log, d_skip, norm_w, B, S):
    T = B * S
    L = SSD_CHUNK
    nc = S // L
    pad8 = lambda v: jnp.pad(v, (0, LANE - SSD_HEADS)).reshape(1, LANE)
    row = lambda b, c: (b * nc + c, 0)
    fixed = lambda b, c: (0, 0)
    return pl.pallas_call(
        _ssd_kernel,
        grid=(B, nc),
        in_specs=[pl.BlockSpec((L, SSD_WIDTH), row),
                  pl.BlockSpec((L, SSD_CONV_DIM), row),
                  pl.BlockSpec((L, LANE), row),
                  pl.BlockSpec((SSD_CONV, SSD_CONV_DIM), fixed),
                  pl.BlockSpec((1, SSD_CONV_DIM), fixed),
                  pl.BlockSpec((1, LANE), fixed),
                  pl.BlockSpec((1, LANE), fixed),
                  pl.BlockSpec((1, SSD_WIDTH), fixed),
                  pl.BlockSpec((1, SSD_WIDTH), fixed)],
        out_specs=pl.BlockSpec((L, SSD_WIDTH), row),
        out_shape=jax.ShapeDtypeStruct((T, SSD_WIDTH), F32),
        scratch_shapes=[pltpu.VMEM((SUBLANE + L, SSD_CONV_DIM), F32),
                        pltpu.VMEM((SSD_HEADS // 2, SSD_STATE, LANE), F32)],
        compiler_params=_cparams("parallel", "arbitrary"),
    )(z, xbc, dt, conv_w, conv_b.reshape(1, -1), pad8(dt_bias), pad8(a_log),
      jnp.repeat(d_skip, HEAD_DIM).reshape(1, SSD_WIDTH), norm_w.reshape(1, SSD_WIDTH))


SB_LOG_CUTOFF = -104.0


def _sb_kernel(q_ref, k_ref, v_ref, o_ref, cf_ref):
    tq = Q_TILE
    H = SB_HEADS
    qi = pl.program_id(1)
    row = _iota((tq, tq), 0)
    col = _iota((tq, tq), 1)
    strict = col < row
    r2 = _iota((tq, 2 * tq), 0)
    c2 = _iota((tq, 2 * tq), 1)
    umat = ((r2 > c2) | (c2 >= tq)).astype(BF16)
    scale = HEAD_DIM ** -0.5
    o_ref[...] = jnp.zeros_like(o_ref)
    cf_ref[...] = jnp.zeros_like(cf_ref)

    def cond(c):
        i, live = c
        return jnp.logical_and(i <= qi, live > 0)

    def body(c):
        i, _ = c
        ks = pl.multiple_of((qi - i) * tq, tq)
        mask = jnp.logical_or(i > 0, strict)[None]
        head_lanes = [slice(h * SLOT, (h + 1) * SLOT) for h in range(H)]
        z = jnp.concatenate([_dot_nt(q_ref[:, ln], k_ref[pl.ds(ks, tq), ln]) for ln in head_lanes], axis=0)
        z = (z * scale).reshape(H, tq, tq)
        l1p = jnp.log1p(jnp.exp(-jnp.abs(z)))
        log_beta = jnp.where(mask, jnp.minimum(z, 0.0) - l1p, -jnp.inf).reshape(H * tq, tq)
        log_stay = jnp.where(mask, -jnp.maximum(z, 0.0) - l1p, 0.0).reshape(H * tq, tq)
        r = _dot_split_lhs(log_stay, umat, 2)
        cf = cf_ref[...]
        w = jnp.exp(log_beta + r[:, 0:tq] + cf)
        for h, ln in enumerate(head_lanes):
            o_ref[:, ln] += _dot(w[h * tq:(h + 1) * tq, :], v_ref[pl.ds(ks, tq), ln])
        cf = cf + r[:, tq:2 * tq]
        cf_ref[...] = cf
        live = (jnp.max(cf) > SB_LOG_CUTOFF).astype(jnp.int32)
        return i + 1, live

    lax.while_loop(cond, body, (jnp.int32(0), jnp.int32(1)))


def _sb(sbq, sbk, sbv, B, S):
    T = B * S
    nq = S // Q_TILE
    W = SB_HEADS * SLOT
    return pl.pallas_call(
        _sb_kernel,
        grid=(B, nq),
        in_specs=[pl.BlockSpec((Q_TILE, W), lambda b, i: (b * nq + i, 0)),
                  pl.BlockSpec((S, W), lambda b, i: (b, 0)),
                  pl.BlockSpec((S, W), lambda b, i: (b, 0))],
        out_specs=pl.BlockSpec((Q_TILE, W), lambda b, i: (b * nq + i, 0)),
        out_shape=jax.ShapeDtypeStruct((T, W), F32),
        scratch_shapes=[pltpu.VMEM((SB_HEADS * Q_TILE, SLOT), F32)],
        compiler_params=_cparams("parallel", "arbitrary"),
    )(sbq, sbk, sbv)


def _rope_tables(S):
    pos = jnp.arange(S, dtype=F32)
    inv_freq = ROPE_THETA ** (-jnp.arange(0, ROT_DIM, 2, dtype=F32) / ROT_DIM)
    ang = pos[:, None] * inv_freq[None, :]
    cos, sin = jnp.cos(ang), jnp.sin(ang)
    half = ROT_DIM // 2
    ones = jnp.ones((S, SLOT - ROT_DIM), F32)
    zeros_h = jnp.zeros((S, half), F32)
    zeros_r = jnp.zeros((S, SLOT - ROT_DIM), F32)
    cosf = jnp.concatenate([cos, cos, ones], axis=1)
    sin_a = jnp.concatenate([-sin, zeros_h, zeros_r], axis=1)
    sin_b = jnp.concatenate([zeros_h, sin, zeros_r], axis=1)
    return cosf, sin_a, sin_b


def _nsa_prep_kernel(q_ref, kv_ref, cos_ref, sa_ref, sb_ref, qw_ref, kw_ref,
                     qo_ref, kc_ref, vc_ref, ksw_ref):
    cos, sa, sb = cos_ref[...], sa_ref[...], sb_ref[...]
    half = ROT_DIM // 2

    def rope(x):
        return x * cos + pltpu.roll(x, SLOT - half, 1) * sa + pltpu.roll(x, half, 1) * sb

    qw = qw_ref[...]
    for h in range(NSA_HEADS):
        lanes = slice(h * SLOT, (h + 1) * SLOT)
        qo_ref[:, lanes] = rope(_rms(q_ref[:, lanes], qw, HEAD_DIM)).astype(BF16)
    kc_ref[...] = rope(kv_ref[:, 0:SLOT])
    vc_ref[...] = kv_ref[:, SLOT:2 * SLOT]
    ksw_ref[:, 0:SLOT] = rope(_rms(kv_ref[:, 2 * SLOT:3 * SLOT], kw_ref[1:2, :], HEAD_DIM)).astype(BF16)
    ksw_ref[:, SLOT:2 * SLOT] = kv_ref[:, 3 * SLOT:4 * SLOT].astype(BF16)
    ksw_ref[:, 2 * SLOT:3 * SLOT] = rope(_rms(kv_ref[:, 4 * SLOT:5 * SLOT], kw_ref[2:3, :], HEAD_DIM)).astype(BF16)
    ksw_ref[:, 3 * SLOT:4 * SLOT] = kv_ref[:, 5 * SLOT:6 * SLOT].astype(BF16)


def _nsa_prep(nq, nkv, ropes, q_norm_w, k_norm_w, B, S):
    T = B * S
    ts = ROW_TILE
    nt = S // ts
    cosf, sin_a, sin_b = ropes
    pad = lambda v: jnp.pad(v, ((0, 0), (0, SLOT - HEAD_DIM)))
    row = lambda b, i: (b * nt + i, 0)
    pos = lambda b, i: (i, 0)
    fixed = lambda b, i: (0, 0)
    return pl.pallas_call(
        _nsa_prep_kernel,
        grid=(B, nt),
        in_specs=[pl.BlockSpec((ts, NSA_HEADS * SLOT), row),
                  pl.BlockSpec((ts, 6 * SLOT), row),
                  pl.BlockSpec((ts, SLOT), pos),
                  pl.BlockSpec((ts, SLOT), pos),
                  pl.BlockSpec((ts, SLOT), pos),
                  pl.BlockSpec((1, SLOT), fixed),
                  pl.BlockSpec((3, SLOT), fixed)],
        out_specs=[pl.BlockSpec((ts, NSA_HEADS * SLOT), row),
                   pl.BlockSpec((ts, SLOT), row),
                   pl.BlockSpec((ts, SLOT), row),
                   pl.BlockSpec((ts, 4 * SLOT), row)],
        out_shape=[jax.ShapeDtypeStruct((T, NSA_HEADS * SLOT), BF16),
                   jax.ShapeDtypeStruct((T, SLOT), F32),
                   jax.ShapeDtypeStruct((T, SLOT), F32),
                   jax.ShapeDtypeStruct((T, 4 * SLOT), BF16)],
        compiler_params=_cparams("parallel", "parallel"),
    )(nq, nkv, cosf, sin_a, sin_b, pad(q_norm_w.reshape(1, HEAD_DIM)), pad(k_norm_w))


def _nsa_cmp_kernel(kc_ref, vc_ref, pos_ref, w_ref, knw_ref, ko_ref, vo_ref):
    nseg = ko_ref.shape[0]
    half = CMP_LEN // 2
    for which, (src, dst) in enumerate(((kc_ref, ko_ref), (vc_ref, vo_ref))):
        f1 = jnp.zeros((nseg, SLOT), F32)
        f2 = jnp.zeros((nseg, SLOT), F32)
        for p in range(half):
            rows = src[pl.ds(p, nseg, stride=CMP_STRIDE), :]
            f1 = f1 + _dot(rows + pos_ref[which, p:p + 1, :], w_ref[which, p])
            f2 = f2 + _dot(rows + pos_ref[which, half + p:half + p + 1, :], w_ref[which, half + p])
        pre = f1 + pltpu.roll(f2, nseg - 1, 0)
        if which == 0:
            pre = _rms(pre, knw_ref[...], HEAD_DIM)
        dst[...] = pre


def _nsa_cmp(kc, vc, cmp_pos, cmp_w, k_norm_w0, B, S):
    nseg = S // CMP_STRIDE
    pos = jnp.pad(cmp_pos, ((0, 0), (0, 0), (0, SLOT - HEAD_DIM)))
    w = cmp_w.reshape(2, CMP_LEN, HEAD_DIM, HEAD_DIM)
    w = jnp.pad(w, ((0, 0), (0, 0), (0, SLOT - HEAD_DIM), (0, SLOT - HEAD_DIM))).astype(BF16)
    knw = jnp.pad(k_norm_w0.reshape(1, HEAD_DIM), ((0, 0), (0, SLOT - HEAD_DIM)))
    per_b = lambda b: (b, 0)
    return pl.pallas_call(
        _nsa_cmp_kernel,
        grid=(B,),
        in_specs=[pl.BlockSpec((S, SLOT), per_b),
                  pl.BlockSpec((S, SLOT), per_b),
                  pl.BlockSpec((2, CMP_LEN, SLOT), lambda b: (0, 0, 0)),
                  pl.BlockSpec((2, CMP_LEN, SLOT, SLOT), lambda b: (0, 0, 0, 0)),
                  pl.BlockSpec((1, SLOT), lambda b: (0, 0))],
        out_specs=[pl.BlockSpec((nseg, SLOT), per_b), pl.BlockSpec((nseg, SLOT), per_b)],
        out_shape=[jax.ShapeDtypeStruct((B * nseg, SLOT), F32),
                   jax.ShapeDtypeStruct((B * nseg, SLOT), F32)],
        compiler_params=_cparams("parallel"),
    )(kc, vc, pos, w, knw)


def _nsa_attn_kernel(q_ref, kc_ref, vc_ref, ksw_ref, g_ref, ow_ref, o_ref, *, S):
    tq = Q_TILE
    n_cmp_pad = S // CMP_STRIDE
    n_sel = S // SEL_BLOCK
    n_top = min(SEL_TOPK, n_sel)
    n_win = WINDOW + tq
    scale = HEAD_DIM ** -0.5
    qi = pl.program_id(1)
    t0 = qi * tq

    t_c = t0 + _iota((tq, n_cmp_pad), 0)
    c_c = _iota((tq, n_cmp_pad), 1)
    vis = (c_c * CMP_STRIDE + (CMP_LEN - 1)) <= t_c
    kc = kc_ref[...]
    vc = vc_ref[...]
    H = NSA_HEADS
    q4 = jnp.concatenate([q_ref[:, h * SLOT:(h + 1) * SLOT] for h in range(H)], axis=0)

    q4s = q4 * scale

    def attend(k, v, ok, nkeys):
        s = jnp.where(ok[None], _dot_nt(q4s, k).reshape(H, tq, nkeys), NEG_BIG)
        e = jnp.exp(s - jnp.max(s, axis=-1, keepdims=True)).reshape(H * tq, nkeys)
        return _dot(e, v) / jnp.sum(e, axis=-1, keepdims=True)

    s = _dot_nt(q4s, kc).reshape(H, tq, n_cmp_pad)
    p = jnp.where(vis[None], _softmax_rows(jnp.where(vis[None], s, NEG_BIG)), 0.0)
    o_cmp = _dot(p.reshape(H * tq, n_cmp_pad), vc)
    p_sum = jnp.sum(p, axis=0)

    r_o = _iota((n_cmp_pad, LANE), 0) * CMP_STRIDE
    j_o = _iota((n_cmp_pad, LANE), 1) * SEL_BLOCK
    overlap = ((r_o < j_o + SEL_BLOCK) & (r_o + CMP_LEN > j_o)
               & (_iota((n_cmp_pad, LANE), 0) < n_cmp_pad - 1)).astype(BF16)
    imp_t = _dot_split_lhs(p_sum, overlap).T
    j_t = _iota((LANE, tq), 0)
    t_t = t0 + _iota((LANE, tq), 1)
    blk = t_t // SEL_BLOCK
    forced = (j_t == 0) | (j_t == blk) | (j_t == blk - 1)
    causal = (j_t * SEL_BLOCK <= t_t) & (j_t < n_sel)
    val = jnp.where(causal, jnp.where(forced, FORCE_SCORE, imp_t), -jnp.inf)
    rank = jnp.zeros((LANE, tq), jnp.int32)
    for jp in range(n_sel):
        vj = val[jp:jp + 1, :]
        beats = (vj > val) | ((vj == val) & (j_t > jp))
        rank = rank + beats.astype(jnp.int32)
    sel_t = (causal & (rank < n_top)).astype(F32)
    sel = sel_t.T.astype(BF16)
    expand = (_iota((LANE, S), 0) == _iota((LANE, S), 1) // SEL_BLOCK).astype(BF16)
    sel_keys = jnp.dot(sel, expand, preferred_element_type=F32)

    t_s = t0 + _iota((tq, S), 0)
    k_s = _iota((tq, S), 1)
    ok_sel = (sel_keys > 0.5) & (k_s <= t_s)
    o_sel = attend(ksw_ref[:, 0:SLOT], ksw_ref[:, SLOT:2 * SLOT], ok_sel, S)

    start = pl.multiple_of(jnp.clip(t0 - WINDOW, 0, S - n_win), tq)
    k_win = ksw_ref[pl.ds(start, n_win), 2 * SLOT:3 * SLOT]
    v_win = ksw_ref[pl.ds(start, n_win), 3 * SLOT:4 * SLOT]
    rel = (t0 + _iota((tq, n_win), 0)) - (start + _iota((tq, n_win), 1))
    ok_win = (rel >= 0) & (rel < WINDOW)
    o_win = attend(k_win, v_win, ok_win, n_win)

    gate = _sigmoid(g_ref[...])
    n_g = 3 * NSA_HEADS
    e_g = (_iota((LANE, n_g * SLOT), 0) == _iota((LANE, n_g * SLOT), 1) // SLOT).astype(BF16)
    g_e = _dot_split_lhs(gate, e_g)
    outs = []
    for h in range(NSA_HEADS):
        g0 = g_e[:, (3 * h) * SLOT:(3 * h + 1) * SLOT]
        g1 = g_e[:, (3 * h + 1) * SLOT:(3 * h + 2) * SLOT]
        g2 = g_e[:, (3 * h + 2) * SLOT:(3 * h + 3) * SLOT]
        rows = slice(h * tq, (h + 1) * tq)
        outs.append(g0 * o_cmp[rows, :] + g1 * o_sel[rows, :] + g2 * o_win[rows, :])
    o = jnp.concatenate(outs, axis=1)
    o_ref[...] = _rms(o, ow_ref[...], NSA_WIDTH)


def _nsa_attn(q_r, k_cmp, v_cmp, ksw, gate, out_w, B, S):
    T = B * S
    nq = S // Q_TILE
    nseg = S // CMP_STRIDE
    row = lambda b, i: (b * nq + i, 0)
    per_b = lambda b, i: (b, 0)
    return pl.pallas_call(
        functools.partial(_nsa_attn_kernel, S=S),
        grid=(B, nq),
        in_specs=[pl.BlockSpec((Q_TILE, NSA_HEADS * SLOT), row),
                  pl.BlockSpec((nseg, SLOT), per_b),
                  pl.BlockSpec((nseg, SLOT), per_b),
                  pl.BlockSpec((S, 4 * SLOT), per_b),
                  pl.BlockSpec((Q_TILE, LANE), row),
                  pl.BlockSpec((1, NSA_HEADS * SLOT), lambda b, i: (0, 0))],
        out_specs=pl.BlockSpec((Q_TILE, NSA_HEADS * SLOT), row),
        out_shape=jax.ShapeDtypeStruct((T, NSA_HEADS * SLOT), F32),
        compiler_params=_cparams("parallel", "parallel"),
    )(q_r, k_cmp, v_cmp, ksw, gate, _pad_heads_vec(out_w, NSA_HEADS))


def _outproj_kernel(x_ref, yssd_ref, ysb_ref, ynsa_ref, sbw_ref, w_ref, o_ref):
    ysb = _rms(ysb_ref[...], sbw_ref[...], SB_WIDTH)
    w0 = SSD_WIDTH
    w1 = w0 + SB_HEADS * SLOT
    w2 = w1 + NSA_HEADS * SLOT
    acc = x_ref[...]
    acc = acc + jnp.dot(yssd_ref[...].astype(BF16), w_ref[0:w0, :], preferred_element_type=F32)
    acc = acc + jnp.dot(ysb.astype(BF16), w_ref[w0:w1, :], preferred_element_type=F32)
    acc = acc + jnp.dot(ynsa_ref[...].astype(BF16), w_ref[w1:w2, :], preferred_element_type=F32)
    o_ref[...] = acc


def _outproj_rowmap():
    src = list(range(SSD_WIDTH))
    for base, nheads in ((SSD_WIDTH, SB_HEADS), (SSD_WIDTH + SB_WIDTH, NSA_HEADS)):
        for h in range(nheads):
            src.extend(range(base + h * HEAD_DIM, base + (h + 1) * HEAD_DIM))
            src.extend([-1] * (SLOT - HEAD_DIM))
    return np.asarray(src, np.int32)


_OUT_ROWMAP = _outproj_rowmap()


def _outproj(xt, y_ssd, y_sb, y_nsa, sb_out_w, w_pad):
    T = xt.shape[0]
    tm = ROW_TILE
    kdim = _OUT_ROWMAP.shape[0]
    row = lambda i: (i, 0)
    fixed = lambda i: (0, 0)
    return pl.pallas_call(
        _outproj_kernel,
        grid=(T // tm,),
        in_specs=[pl.BlockSpec((tm, D_MODEL), row),
                  pl.BlockSpec((tm, SSD_WIDTH), row),
                  pl.BlockSpec((tm, SB_HEADS * SLOT), row),
                  pl.BlockSpec((tm, NSA_HEADS * SLOT), row),
                  pl.BlockSpec((1, SB_HEADS * SLOT), fixed),
                  pl.BlockSpec((kdim, D_MODEL), fixed)],
        out_specs=pl.BlockSpec((tm, D_MODEL), row),
        out_shape=jax.ShapeDtypeStruct((T, D_MODEL), F32),
        compiler_params=_cparams("parallel"),
    )(xt, y_ssd, y_sb, y_nsa, _pad_heads_vec(sb_out_w, SB_HEADS), w_pad)


def _cmp_exchange(x, y):
    if y is None:
        return x, None
    if x is None:
        return y, None
    return jnp.maximum(x, y), jnp.minimum(x, y)


def _bitonic_sort_desc(u):
    u = list(u)
    n = len(u)
    k = 2
    while k <= n:
        j = k // 2
        while j >= 1:
            for i in range(n):
                l = i ^ j
                if l > i:
                    hi, lo = _cmp_exchange(u[i], u[l])
                    u[i], u[l] = (hi, lo) if (i & k) == 0 else (lo, hi)
            j //= 2
        k *= 2
    return u


def _merge_top(a, b):
    n = len(a)
    c = [_cmp_exchange(a[i], b[n - 1 - i])[0] for i in range(n)]
    d = n // 2
    while d >= 1:
        for i in range(n):
            if (i & d) == 0:
                c[i], c[i + d] = _cmp_exchange(c[i], c[i + d])
        d //= 2
    return c


def _top16_replicated(s):
    groups = s.shape[0] // SUBLANE
    u = _bitonic_sort_desc([s[g * SUBLANE:(g + 1) * SUBLANE, :] for g in range(groups)])
    shift = SUBLANE // 2
    while shift >= 1:
        u = _merge_top(u, [pltpu.roll(x, shift, 0) for x in u])
        shift //= 2
    return u


def _bf16_pair_words(x):
    bits = lax.bitcast_convert_type(x.astype(BF16).astype(F32), jnp.uint32)
    return bits | (bits >> 16)


def _peer_select_kernel(x_ref, nw_ref, wq_ref, sk_ref,
                        h_ref, r_ref, e1_ref, rk_ref, e2_ref):
    K = PEER_TOPK
    half = PEER_QDIM // 2
    nk = N_KEYS
    h = _rms(x_ref[...], nw_ref[...], D_MODEL)
    h_ref[...] = h.T.astype(BF16)
    q_t = _dot_nt(wq_ref[...], h)
    groups = nk // SUBLANE
    tokens = q_t.shape[1]
    for hd, t0 in [(hd, t0) for hd in range(PEER_HEADS) for t0 in range(0, tokens, LANE)]:
        lanes = slice(t0, t0 + LANE)
        r0 = hd * 2 * half
        if t0 == 0:
            s1_all = _dot(sk_ref[0], q_t[r0:r0 + half, :])
            s2_all = _dot(sk_ref[1], q_t[r0 + half:r0 + 2 * half, :])
        s1, s2 = s1_all[:, lanes], s2_all[:, lanes]
        sv1 = _top16_replicated(s1)
        sv2 = _top16_replicated(s2)
        lists = []
        for a in range(K):
            row = [sv1[a] + sv2[b] for b in range(K // (a + 1))]
            lists.append(row + [None] * (K - len(row)))
        while len(lists) > 1:
            nxt = [_merge_top(lists[i], lists[i + 1]) for i in range(0, len(lists) - 1, 2)]
            if len(lists) % 2:
                nxt.append(lists[-1])
            lists = nxt
        cv = lists[0]
        tau = cv[K - 1]
        zsum = None
        for k in range(K):
            e = jnp.exp(cv[k] - cv[0])
            zsum = e if zsum is None else zsum + e
        inv_z = 1.0 / zsum
        r_parts, rk_parts, e1_parts, e2_parts = [], [], [], []
        for g in range(groups):
            rows = slice(g * SUBLANE, (g + 1) * SUBLANE)
            a1, a2 = s1[rows, :], s2[rows, :]
            cnt = jnp.full(a1.shape, -1.0, F32)
            rk = jnp.zeros(a2.shape, F32)
            for b in range(K):
                cnt = cnt + jnp.where(a1 + sv2[b] >= tau, 1.0, 0.0)
                rk = rk + jnp.where(sv2[b] > a2, 1.0, 0.0)
            r_parts.append(cnt)
            rk_parts.append(rk)
            e1_parts.append(jnp.exp(a1 - sv1[0]))
            e2_parts.append(jnp.exp(a2 - sv2[0]) * inv_z)
        r_ref[hd, :, lanes] = _bf16_pair_words(jnp.concatenate(r_parts, axis=0))
        e1_ref[hd, :, lanes] = _bf16_pair_words(jnp.concatenate(e1_parts, axis=0))
        rk_ref[hd, :, lanes] = jnp.concatenate(rk_parts, axis=0).astype(BF16)
        e2_ref[hd, :, lanes] = jnp.concatenate(e2_parts, axis=0).astype(BF16)


BF16_ROWS = 2 * SUBLANE
GATE_LANES = 2 * LANE


def _peer_dense_kernel(h_ref, x_ref, u_ref, vt_ref, r_ref, e1_ref, rk_ref, e2_ref,
                       o_ref, acc_ref, p_ref):
    c = pl.program_id(1)
    nk = N_KEYS
    tb = h_ref.shape[1]

    @pl.when(c == 0)
    def _():
        acc_ref[...] = jnp.zeros_like(acc_ref)

    h_t = jnp.dot(u_ref[...], h_ref[...], preferred_element_type=F32)
    row0 = pl.multiple_of(c * SUBLANE, SUBLANE)
    r_rows = [r_ref[hd, pl.ds(row0, SUBLANE), :] for hd in range(PEER_HEADS)]
    e1_rows = [e1_ref[hd, pl.ds(row0, SUBLANE), :] for hd in range(PEER_HEADS)]
    tw = GATE_LANES
    for t0 in range(0, tb, tw):
        lanes = slice(t0, t0 + tw)
        for il in range(SUBLANE):
            a = h_t[il * nk:(il + 1) * nk, lanes]
            act = (0.5 * a * (1.0 + lax.erf(a * (2.0 ** -0.5)))).astype(BF16)
            rb = [pltpu.bitcast(jnp.broadcast_to(r_rows[hd][il:il + 1, lanes], (SUBLANE, tw)), BF16)
                  for hd in range(PEER_HEADS)]
            eb = [pltpu.bitcast(jnp.broadcast_to(e1_rows[hd][il:il + 1, lanes], (SUBLANE, tw)), BF16)
                  for hd in range(PEER_HEADS)]
            for j in range(nk // BF16_ROWS):
                rows = pl.ds(j * BF16_ROWS, BF16_ROWS)
                w = None
                for hd in range(PEER_HEADS):
                    t = jnp.where(rk_ref[hd, rows, lanes] <= rb[hd], e2_ref[hd, rows, lanes],
                                  jnp.zeros((), BF16)) * eb[hd]
                    w = t if w is None else w + t
                p_ref[pl.ds(il * nk + j * BF16_ROWS, BF16_ROWS), lanes] = (
                    w * act[j * BF16_ROWS:(j + 1) * BF16_ROWS, :])
    acc_ref[...] += jnp.dot(vt_ref[...], p_ref[...], preferred_element_type=F32)

    @pl.when(c == pl.num_programs(1) - 1)
    def _():
        o_ref[...] = x_ref[...] + acc_ref[...].T


def _peer(x1, norm_w, wq_t, sub_keys, u_bf, vt_bf):
    T = x1.shape[0]
    H, nk = PEER_HEADS, N_KEYS
    tb_s = SELECT_TILE
    row = lambda i: (i, 0)
    fixed2 = lambda i: (0, 0)
    tok3 = lambda i: (0, 0, i)
    head_f32 = jax.ShapeDtypeStruct((H, nk, T), jnp.uint32)
    head_bf16 = jax.ShapeDtypeStruct((H, nk, T), BF16)
    h_bf, r_cnt, e1, rk2, e2 = pl.pallas_call(
        _peer_select_kernel,
        grid=(T // tb_s,),
        in_specs=[pl.BlockSpec((tb_s, D_MODEL), row),
                  pl.BlockSpec((1, D_MODEL), fixed2),
                  pl.BlockSpec((H * PEER_QDIM, D_MODEL), fixed2),
                  pl.BlockSpec((2, nk, PEER_QDIM // 2), lambda i: (0, 0, 0))],
        out_specs=[pl.BlockSpec((D_MODEL, tb_s), lambda i: (0, i))] + [pl.BlockSpec((H, nk, tb_s), tok3)] * 4,
        out_shape=[jax.ShapeDtypeStruct((D_MODEL, T), BF16), head_f32, head_f32, head_bf16, head_bf16],
        compiler_params=_cparams("parallel"),
    )(x1, norm_w.reshape(1, D_MODEL), wq_t, sub_keys)

    tb = DENSE_TILE
    chunk = SUBLANE
    once = pl.Buffered(1)
    gate_spec = pl.BlockSpec((H, nk, tb), lambda i, c: (0, 0, i), pipeline_mode=once)
    return pl.pallas_call(
        _peer_dense_kernel,
        grid=(T // tb, nk // chunk),
        in_specs=[pl.BlockSpec((D_MODEL, tb), lambda i, c: (0, i), pipeline_mode=once),
                  pl.BlockSpec((tb, D_MODEL), lambda i, c: (i, 0), pipeline_mode=once),
                  pl.BlockSpec((chunk * nk, D_MODEL), lambda i, c: (c, 0)),
                  pl.BlockSpec((D_MODEL, chunk * nk), lambda i, c: (0, c))]
                 + [gate_spec] * 4,
        out_specs=pl.BlockSpec((tb, D_MODEL), lambda i, c: (i, 0)),
        out_shape=jax.ShapeDtypeStruct((T, D_MODEL), F32),
        scratch_shapes=[pltpu.VMEM((D_MODEL, tb), F32),
                        pltpu.VMEM((chunk * nk, tb), BF16)],
        compiler_params=_cparams("parallel", "arbitrary"),
    )(h_bf, x1, u_bf, vt_bf, r_cnt, e1, rk2, e2)


def kernel(x, norm1_w, w_in, conv_w, conv_b, dt_bias, a_log, d_skip, ssd_norm_w, sb_out_w,
           nsa_q_norm_w, nsa_k_norm_w, nsa_cmp_pos, nsa_cmp_w, nsa_out_w, w_out, norm2_w,
           peer_w_q, peer_sub_keys, peer_u, peer_v):
    B, S, _ = x.shape
    depth = w_in.shape[0]
    assert S == 2048, "NSA tiling is laid out for 2048-token sequences"
    T = B * S
    ropes = _rope_tables(S)
    xt = x.reshape(T, D_MODEL)
    for l in range(depth):
        w_in_pad = _pad_cols(w_in[l], _IN_COLMAP).astype(BF16)
        z, xbc, dt, sbq, sbk, sbv, nq, nkv, gate = _inproj(xt, norm1_w[l], w_in_pad)
        y_ssd = _ssd(z, xbc, dt, conv_w[l], conv_b[l], dt_bias[l], a_log[l], d_skip[l], ssd_norm_w[l], B, S)
        y_sb = _sb(sbq, sbk, sbv, B, S)
        q_r, kc, vc, ksw = _nsa_prep(nq, nkv, ropes, nsa_q_norm_w[l], nsa_k_norm_w[l], B, S)
        k_cmp, v_cmp = _nsa_cmp(kc, vc, nsa_cmp_pos[l], nsa_cmp_w[l], nsa_k_norm_w[l, 0], B, S)
        y_nsa = _nsa_attn(q_r, k_cmp, v_cmp, ksw, gate, nsa_out_w[l], B, S)
        w_out_pad = _pad_cols(w_out[l].T, _OUT_ROWMAP).T.astype(BF16)
        x1 = _outproj(xt, y_ssd, y_sb, y_nsa, sb_out_w[l], w_out_pad)
        xt = _peer(x1, norm2_w[l], peer_w_q[l].T.astype(BF16), peer_sub_keys[l],
                   peer_u[l].astype(BF16), peer_v[l].T.astype(BF16))
    return xt.reshape(B, S, D_MODEL)
```

```python
import functools
import math

import numpy as np
import jax
import jax.numpy as jnp
from jax import lax
from jax.experimental import pallas as pl
from jax.experimental.pallas import tpu as pltpu

F32 = jnp.float32
BF16 = jnp.bfloat16

D_MODEL = 1024
HEAD_DIM = 64
SSD_WIDTH = 512
SSD_HEADS = 8
SSD_GROUPS = 2
SSD_STATE = 128
SSD_CONV = 4
SSD_CHUNK = 128
SSD_CONV_DIM = SSD_WIDTH + 2 * SSD_GROUPS * SSD_STATE
SB_WIDTH = 256
SB_HEADS = 4
NSA_WIDTH = 256
NSA_HEADS = 4
CMP_LEN = 32
CMP_STRIDE = 16
SEL_BLOCK = 64
SEL_TOPK = 16
WINDOW = 512
FORCE_SCORE = 1e9
ROT_DIM = 16
ROPE_THETA = 500000.0
PEER_HEADS = 8
N_KEYS = 128
PEER_TOPK = 16
PEER_QDIM = 256
NORM_EPS = 1e-6
NEG_BIG = -1e30

LANE = 128
SUBLANE = 8
V7X_VMEM_BYTES = 64 * 1024 * 1024
VMEM_LIMIT = V7X_VMEM_BYTES - 12 * 1024 * 1024

SLOT = LANE

Q_TILE = 128
ROW_TILE = 512
SELECT_TILE = 256
DENSE_TILE = 1024


def _cparams(*sem, flags=None):
    return pltpu.CompilerParams(dimension_semantics=sem, vmem_limit_bytes=VMEM_LIMIT, flags=flags)


def _iota(shape, dim):
    return lax.broadcasted_iota(jnp.int32, shape, dim)


def _dot(a, b):
    return jnp.dot(a.astype(BF16), b.astype(BF16), preferred_element_type=F32)


def _dot_nt(a, b):
    return lax.dot_general(a.astype(BF16), b.astype(BF16), (((1,), (1,)), ((), ())),
                           preferred_element_type=F32)


def _split(a, n):
    parts, r = [], a
    for i in range(n):
        p = r.astype(BF16)
        parts.append(p)
        if i + 1 < n:
            r = r - p.astype(F32)
    return parts


def _dot_split_lhs(a, b_exact, n=3):
    acc = None
    for p in _split(a, n):
        t = jnp.dot(p, b_exact, preferred_element_type=F32)
        acc = t if acc is None else acc + t
    return acc


def _dot_split_rhs(a_exact, b, n=3):
    acc = None
    for p in _split(b, n):
        t = jnp.dot(a_exact, p, preferred_element_type=F32)
        acc = t if acc is None else acc + t
    return acc


def _sigmoid(x):
    return 1.0 / (1.0 + jnp.exp(-x))


def _softplus(x):
    return jnp.maximum(x, 0.0) + jnp.log1p(jnp.exp(-jnp.abs(x)))


def _rms(x, w, n):
    ms = jnp.sum(x * x, axis=-1, keepdims=True) * (1.0 / n)
    return x * lax.rsqrt(ms + NORM_EPS) * w


def _softmax_rows(s):
    m = jnp.max(s, axis=-1, keepdims=True)
    e = jnp.exp(s - m)
    return e / jnp.sum(e, axis=-1, keepdims=True)


_IN_SEGS = (("z", 512, F32), ("xbc", 1024, F32), ("dt", 128, F32),
            ("sbq", 512, BF16), ("sbk", 512, BF16), ("sbv", 512, BF16),
            ("nq", 512, F32), ("nkv", 768, F32), ("gate", 128, F32))
_IN_TOTAL = sum(s[1] for s in _IN_SEGS)


def _inproj_colmap():
    src = []

    def real(a, n, pad_to=None):
        src.extend(range(a, a + n))
        if pad_to:
            src.extend([-1] * (pad_to - n))

    def heads(a, nheads):
        for h in range(nheads):
            src.extend(range(a + h * HEAD_DIM, a + (h + 1) * HEAD_DIM))
            src.extend([-1] * (SLOT - HEAD_DIM))

    o_xbc = SSD_WIDTH
    o_dt = o_xbc + SSD_CONV_DIM
    o_sb = o_dt + SSD_HEADS
    o_nq = o_sb + 3 * SB_WIDTH
    o_nkv = o_nq + NSA_WIDTH
    o_gate = o_nkv + 6 * HEAD_DIM
    real(0, SSD_WIDTH)
    real(o_xbc, SSD_CONV_DIM)
    real(o_dt, SSD_HEADS, LANE)
    heads(o_sb, SB_HEADS)
    heads(o_sb + SB_WIDTH, SB_HEADS)
    heads(o_sb + 2 * SB_WIDTH, SB_HEADS)
    heads(o_nq, NSA_HEADS)
    heads(o_nkv, 6)
    real(o_gate, 3 * NSA_HEADS, LANE)
    src = np.asarray(src, np.int32)
    assert src.shape[0] == _IN_TOTAL
    return src


_IN_COLMAP = _inproj_colmap()


def _pad_cols(w, colmap):
    parts, i, n = [], 0, len(colmap)
    while i < n:
        j = i + 1
        if colmap[i] < 0:
            while j < n and colmap[j] < 0:
                j += 1
            parts.append(jnp.zeros(w.shape[:-1] + (j - i,), w.dtype))
        else:
            while j < n and colmap[j] == colmap[j - 1] + 1:
                j += 1
            parts.append(w[..., int(colmap[i]):int(colmap[i]) + (j - i)])
        i = j
    return jnp.concatenate(parts, axis=-1)


def _pad_heads_vec(v, nheads):
    v = v.reshape(nheads, HEAD_DIM)
    v = jnp.pad(v, ((0, 0), (0, SLOT - HEAD_DIM)))
    return v.reshape(1, nheads * SLOT)


def _inproj_kernel(x_ref, nw_ref, w_ref, *out_refs):
    x = x_ref[...]
    h = _rms(x, nw_ref[...], D_MODEL).astype(BF16)
    off = 0
    for (name, width, dt), o_ref in zip(_IN_SEGS, out_refs):
        o_ref[...] = jnp.dot(h, w_ref[:, off:off + width], preferred_element_type=F32).astype(dt)
        off += width


def _inproj(xt, norm_w, w_pad):
    T = xt.shape[0]
    tm = ROW_TILE
    out_shape = [jax.ShapeDtypeStruct((T, w), dt) for (_, w, dt) in _IN_SEGS]
    out_specs = [pl.BlockSpec((tm, w), lambda i: (i, 0)) for (_, w, _) in _IN_SEGS]
    return pl.pallas_call(
        _inproj_kernel,
        grid=(T // tm,),
        in_specs=[pl.BlockSpec((tm, D_MODEL), lambda i: (i, 0)),
                  pl.BlockSpec((1, D_MODEL), lambda i: (0, 0)),
                  pl.BlockSpec((D_MODEL, _IN_TOTAL), lambda i: (0, 0))],
        out_specs=out_specs,
        out_shape=out_shape,
        compiler_params=_cparams("parallel"),
    )(xt, norm_w.reshape(1, D_MODEL), w_pad)


def _ssd_kernel(z_ref, xbc_ref, dt_ref, cw_ref, cb_ref, dtb_ref, alog_ref, dsk_ref, nw_ref,
                o_ref, xbuf, state):
    L = SSD_CHUNK
    c = pl.program_id(1)

    @pl.when(c == 0)
    def _():
        xbuf[0:SUBLANE, :] = jnp.zeros((SUBLANE, SSD_CONV_DIM), F32)
        state[...] = jnp.zeros_like(state)

    xcur = xbc_ref[...]
    xbuf[SUBLANE:SUBLANE + L, :] = xcur
    acc = jnp.broadcast_to(cb_ref[...], (L, SSD_CONV_DIM))
    for k in range(SSD_CONV):
        s0 = SUBLANE - (SSD_CONV - 1) + k
        acc = acc + xbuf[s0:s0 + L, :] * cw_ref[k:k + 1, :]
    xbuf[0:SUBLANE, :] = xcur[L - SUBLANE:L, :]
    xc = acc * _sigmoid(acc)

    xs = xc[:, 0:SSD_WIDTH]
    dt = _softplus(dt_ref[...] + dtb_ref[...])
    a_dec = dt * (-jnp.exp(alog_ref[...]))

    row = _iota((L, L), 0)
    col = _iota((L, L), 1)
    tril = row >= col
    tri = tril.astype(BF16)
    cs = _dot_split_rhs(tri, a_dec)
    e64 = (_iota((LANE, SSD_WIDTH), 0) == _iota((LANE, SSD_WIDTH), 1) // HEAD_DIM).astype(BF16)
    e128 = (_iota((LANE, SSD_HEADS * LANE), 0) == _iota((LANE, SSD_HEADS * LANE), 1) // LANE).astype(BF16)
    dt_e = _dot_split_lhs(dt, e64)
    cs_e = _dot_split_lhs(cs, e64)
    cs_e2 = _dot_split_lhs(cs, e128)
    cs_t = cs.T
    cs_last = cs_e[L - 1:L, :]
    x_dt = xs * dt_e
    x_dec = x_dt * jnp.exp(cs_last - cs_e)
    in_dec = jnp.exp(cs_e)
    chunk_dec = jnp.exp(cs_last)
    lane_lo = _iota((L, LANE), 1) < HEAD_DIM

    ys = []
    for g in range(SSD_GROUPS):
        b_g = xc[:, SSD_WIDTH + g * SSD_STATE:SSD_WIDTH + (g + 1) * SSD_STATE]
        c_g = xc[:, SSD_WIDTH + (SSD_GROUPS + g) * SSD_STATE:SSD_WIDTH + (SSD_GROUPS + g + 1) * SSD_STATE]
        cb = _dot_nt(c_g, b_g)
        b_gt = b_g.T
        for m in range(2 * g, 2 * g + 2):
            lanes = slice(m * LANE, (m + 1) * LANE)
            xp = x_dt[:, lanes]
            yd = []
            for h in (2 * m, 2 * m + 1):
                diff = cs_e2[:, h * LANE:(h + 1) * LANE] - cs_t[h:h + 1, :]
                lm = jnp.exp(jnp.where(tril, diff, -jnp.inf))
                yd.append(_dot(lm * cb, xp))
            y_diag = jnp.where(lane_lo, yd[0], yd[1])
            s_prev = state[m]
            y_off = _dot(c_g, s_prev) * in_dec[:, lanes]
            state[m] = s_prev * chunk_dec[:, lanes] + _dot(b_gt, x_dec[:, lanes])
            ys.append(y_diag + y_off + xs[:, lanes] * dsk_ref[:, lanes])
    y = jnp.concatenate(ys, axis=1)
    zt = z_ref[...]
    o_ref[...] = _rms(y * (zt * _sigmoid(zt)), nw_ref[...], SSD_WIDTH)


def _ssd(z, xbc, dt, conv_w, conv_b, dt_bias, a_log, d_skip, norm_w, B, S):
    T = B * S
    L = SSD_CHUNK
    nc = S // L
    pad8 = lambda v: jnp.pad(v, (0, LANE - SSD_HEADS)).reshape(1, LANE)
    row = lambda b, c: (b * nc + c, 0)
    fixed = lambda b, c: (0, 0)
    return pl.pallas_call(
        _ssd_kernel,
        grid=(B, nc),
        in_specs=[pl.BlockSpec((L, SSD_WIDTH), row),
                  pl.BlockSpec((L, SSD_CONV_DIM), row),
                  pl.BlockSpec((L, LANE), row),
                  pl.BlockSpec((SSD_CONV, SSD_CONV_DIM), fixed),
                  pl.BlockSpec((1, SSD_CONV_DIM), fixed),
                  pl.BlockSpec((1, LANE), fixed),
                  pl.BlockSpec((1, LANE), fixed),
                  pl.BlockSpec((1, SSD_WIDTH), fixed),
                  pl.BlockSpec((1, SSD_WIDTH), fixed)],
        out_specs=pl.BlockSpec((L, SSD_WIDTH), row),
        out_shape=jax.ShapeDtypeStruct((T, SSD_WIDTH), F32),
        scratch_shapes=[pltpu.VMEM((SUBLANE + L, SSD_CONV_DIM), F32),
                        pltpu.VMEM((SSD_HEADS // 2, SSD_STATE, LANE), F32)],
        compiler_params=_cparams("parallel", "arbitrary"),
    )(z, xbc, dt, conv_w, conv_b.reshape(1, -1), pad8(dt_bias), pad8(a_log),
      jnp.repeat(d_skip, HEAD_DIM).reshape(1, SSD_WIDTH), norm_w.reshape(1, SSD_WIDTH))


SB_LOG_CUTOFF = -104.0


def _sb_kernel(q_ref, k_ref, v_ref, o_ref, cf_ref):
    tq = Q_TILE
    H = SB_HEADS
    qi = pl.program_id(1)
    row = _iota((tq, tq), 0)
    col = _iota((tq, tq), 1)
    strict = col < row
    r2 = _iota((tq, 2 * tq), 0)
    c2 = _iota((tq, 2 * tq), 1)
    umat = ((r2 > c2) | (c2 >= tq)).astype(BF16)
    scale = HEAD_DIM ** -0.5
    o_ref[...] = jnp.zeros_like(o_ref)
    cf_ref[...] = jnp.zeros_like(cf_ref)

    def cond(c):
        i, live = c
        return jnp.logical_and(i <= qi, live > 0)

    def body(c):
        i, _ = c
        ks = pl.multiple_of((qi - i) * tq, tq)
        mask = jnp.logical_or(i > 0, strict)[None]
        head_lanes = [slice(h * SLOT, (h + 1) * SLOT) for h in range(H)]
        z = jnp.concatenate([_dot_nt(q_ref[:, ln], k_ref[pl.ds(ks, tq), ln]) for ln in head_lanes], axis=0)
        z = (z * scale).reshape(H, tq, tq)
        l1p = jnp.log1p(jnp.exp(-jnp.abs(z)))
        log_beta = jnp.where(mask, jnp.minimum(z, 0.0) - l1p, -jnp.inf).reshape(H * tq, tq)
        log_stay = jnp.where(mask, -jnp.maximum(z, 0.0) - l1p, 0.0).reshape(H * tq, tq)
        r = _dot_split_lhs(log_stay, umat, 2)
        cf = cf_ref[...]
        w = jnp.exp(log_beta + r[:, 0:tq] + cf)
        for h, ln in enumerate(head_lanes):
            o_ref[:, ln] += _dot(w[h * tq:(h + 1) * tq, :], v_ref[pl.ds(ks, tq), ln])
        cf = cf + r[:, tq:2 * tq]
        cf_ref[...] = cf
        live = (jnp.max(cf) > SB_LOG_CUTOFF).astype(jnp.int32)
        return i + 1, live

    lax.while_loop(cond, body, (jnp.int32(0), jnp.int32(1)))


def _sb(sbq, sbk, sbv, B, S):
    T = B * S
    nq = S // Q_TILE
    W = SB_HEADS * SLOT
    return pl.pallas_call(
        _sb_kernel,
        grid=(B, nq),
        in_specs=[pl.BlockSpec((Q_TILE, W), lambda b, i: (b * nq + i, 0)),
                  pl.BlockSpec((S, W), lambda b, i: (b, 0)),
                  pl.BlockSpec((S, W), lambda b, i: (b, 0))],
        out_specs=pl.BlockSpec((Q_TILE, W), lambda b, i: (b * nq + i, 0)),
        out_shape=jax.ShapeDtypeStruct((T, W), F32),
        scratch_shapes=[pltpu.VMEM((SB_HEADS * Q_TILE, SLOT), F32)],
        compiler_params=_cparams("parallel", "arbitrary"),
    )(sbq, sbk, sbv)


def _rope_tables(S):
    pos = jnp.arange(S, dtype=F32)
    inv_freq = ROPE_THETA ** (-jnp.arange(0, ROT_DIM, 2, dtype=F32) / ROT_DIM)
    ang = pos[:, None] * inv_freq[None, :]
    cos, sin = jnp.cos(ang), jnp.sin(ang)
    half = ROT_DIM // 2
    ones = jnp.ones((S, SLOT - ROT_DIM), F32)
    zeros_h = jnp.zeros((S, half), F32)
    zeros_r = jnp.zeros((S, SLOT - ROT_DIM), F32)
    cosf = jnp.concatenate([cos, cos, ones], axis=1)
    sin_a = jnp.concatenate([-sin, zeros_h, zeros_r], axis=1)
    sin_b = jnp.concatenate([zeros_h, sin, zeros_r], axis=1)
    return cosf, sin_a, sin_b


def _nsa_prep_kernel(q_ref, kv_ref, cos_ref, sa_ref, sb_ref, qw_ref, kw_ref,
                     qo_ref, kc_ref, vc_ref, ksw_ref):
    cos, sa, sb = cos_ref[...], sa_ref[...], sb_ref[...]
    half = ROT_DIM // 2

    def rope(x):
        return x * cos + pltpu.roll(x, SLOT - half, 1) * sa + pltpu.roll(x, half, 1) * sb

    qw = qw_ref[...]
    for h in range(NSA_HEADS):
        lanes = slice(h * SLOT, (h + 1) * SLOT)
        qo_ref[:, lanes] = rope(_rms(q_ref[:, lanes], qw, HEAD_DIM)).astype(BF16)
    kc_ref[...] = rope(kv_ref[:, 0:SLOT])
    vc_ref[...] = kv_ref[:, SLOT:2 * SLOT]
    ksw_ref[:, 0:SLOT] = rope(_rms(kv_ref[:, 2 * SLOT:3 * SLOT], kw_ref[1:2, :], HEAD_DIM)).astype(BF16)
    ksw_ref[:, SLOT:2 * SLOT] = kv_ref[:, 3 * SLOT:4 * SLOT].astype(BF16)
    ksw_ref[:, 2 * SLOT:3 * SLOT] = rope(_rms(kv_ref[:, 4 * SLOT:5 * SLOT], kw_ref[2:3, :], HEAD_DIM)).astype(BF16)
    ksw_ref[:, 3 * SLOT:4 * SLOT] = kv_ref[:, 5 * SLOT:6 * SLOT].astype(BF16)


def _nsa_prep(nq, nkv, ropes, q_norm_w, k_norm_w, B, S):
    T = B * S
    ts = ROW_TILE
    nt = S // ts
    cosf, sin_a, sin_b = ropes
    pad = lambda v: jnp.pad(v, ((0, 0), (0, SLOT - HEAD_DIM)))
    row = lambda b, i: (b * nt + i, 0)
    pos = lambda b, i: (i, 0)
    fixed = lambda b, i: (0, 0)
    return pl.pallas_call(
        _nsa_prep_kernel,
        grid=(B, nt),
        in_specs=[pl.BlockSpec((ts, NSA_HEADS * SLOT), row),
                  pl.BlockSpec((ts, 6 * SLOT), row),
                  pl.BlockSpec((ts, SLOT), pos),
                  pl.BlockSpec((ts, SLOT), pos),
                  pl.BlockSpec((ts, SLOT), pos),
                  pl.BlockSpec((1, SLOT), fixed),
                  pl.BlockSpec((3, SLOT), fixed)],
        out_specs=[pl.BlockSpec((ts, NSA_HEADS * SLOT), row),
                   pl.BlockSpec((ts, SLOT), row),
                   pl.BlockSpec((ts, SLOT), row),
                   pl.BlockSpec((ts, 4 * SLOT), row)],
        out_shape=[jax.ShapeDtypeStruct((T, NSA_HEADS * SLOT), BF16),
                   jax.ShapeDtypeStruct((T, SLOT), F32),
                   jax.ShapeDtypeStruct((T, SLOT), F32),
                   jax.ShapeDtypeStruct((T, 4 * SLOT), BF16)],
        compiler_params=_cparams("parallel", "parallel"),
    )(nq, nkv, cosf, sin_a, sin_b, pad(q_norm_w.reshape(1, HEAD_DIM)), pad(k_norm_w))


def _nsa_cmp_kernel(kc_ref, vc_ref, pos_ref, w_ref, knw_ref, ko_ref, vo_ref):
    nseg = ko_ref.shape[0]
    half = CMP_LEN // 2
    for which, (src, dst) in enumerate(((kc_ref, ko_ref), (vc_ref, vo_ref))):
        f1 = jnp.zeros((nseg, SLOT), F32)
        f2 = jnp.zeros((nseg, SLOT), F32)
        for p in range(half):
            rows = src[pl.ds(p, nseg, stride=CMP_STRIDE), :]
            f1 = f1 + _dot(rows + pos_ref[which, p:p + 1, :], w_ref[which, p])
            f2 = f2 + _dot(rows + pos_ref[which, half + p:half + p + 1, :], w_ref[which, half + p])
        pre = f1 + pltpu.roll(f2, nseg - 1, 0)
        if which == 0:
            pre = _rms(pre, knw_ref[...], HEAD_DIM)
        dst[...] = pre


def _nsa_cmp(kc, vc, cmp_pos, cmp_w, k_norm_w0, B, S):
    nseg = S // CMP_STRIDE
    pos = jnp.pad(cmp_pos, ((0, 0), (0, 0), (0, SLOT - HEAD_DIM)))
    w = cmp_w.reshape(2, CMP_LEN, HEAD_DIM, HEAD_DIM)
    w = jnp.pad(w, ((0, 0), (0, 0), (0, SLOT - HEAD_DIM), (0, SLOT - HEAD_DIM))).astype(BF16)
    knw = jnp.pad(k_norm_w0.reshape(1, HEAD_DIM), ((0, 0), (0, SLOT - HEAD_DIM)))
    per_b = lambda b: (b, 0)
    return pl.pallas_call(
        _nsa_cmp_kernel,
        grid=(B,),
        in_specs=[pl.BlockSpec((S, SLOT), per_b),
                  pl.BlockSpec((S, SLOT), per_b),
                  pl.BlockSpec((2, CMP_LEN, SLOT), lambda b: (0, 0, 0)),
                  pl.BlockSpec((2, CMP_LEN, SLOT, SLOT), lambda b: (0, 0, 0, 0)),
                  pl.BlockSpec((1, SLOT), lambda b: (0, 0))],
        out_specs=[pl.BlockSpec((nseg, SLOT), per_b), pl.BlockSpec((nseg, SLOT), per_b)],
        out_shape=[jax.ShapeDtypeStruct((B * nseg, SLOT), F32),
                   jax.ShapeDtypeStruct((B * nseg, SLOT), F32)],
        compiler_params=_cparams("parallel"),
    )(kc, vc, pos, w, knw)


def _nsa_attn_kernel(q_ref, kc_ref, vc_ref, ksw_ref, g_ref, ow_ref, o_ref, *, S):
    tq = Q_TILE
    n_cmp_pad = S // CMP_STRIDE
    n_sel = S // SEL_BLOCK
    n_top = min(SEL_TOPK, n_sel)
    n_win = WINDOW + tq
    scale = HEAD_DIM ** -0.5
    qi = pl.program_id(1)
    t0 = qi * tq

    t_c = t0 + _iota((tq, n_cmp_pad), 0)
    c_c = _iota((tq, n_cmp_pad), 1)
    vis = (c_c * CMP_STRIDE + (CMP_LEN - 1)) <= t_c
    kc = kc_ref[...]
    vc = vc_ref[...]
    H = NSA_HEADS
    q4 = jnp.concatenate([q_ref[:, h * SLOT:(h + 1) * SLOT] for h in range(H)], axis=0)

    q4s = q4 * scale

    def attend(k, v, ok, nkeys):
        s = jnp.where(ok[None], _dot_nt(q4s, k).reshape(H, tq, nkeys), NEG_BIG)
        e = jnp.exp(s - jnp.max(s, axis=-1, keepdims=True)).reshape(H * tq, nkeys)
        return _dot(e, v) / jnp.sum(e, axis=-1, keepdims=True)

    s = _dot_nt(q4s, kc).reshape(H, tq, n_cmp_pad)
    p = jnp.where(vis[None], _softmax_rows(jnp.where(vis[None], s, NEG_BIG)), 0.0)
    o_cmp = _dot(p.reshape(H * tq, n_cmp_pad), vc)
    p_sum = jnp.sum(p, axis=0)

    r_o = _iota((n_cmp_pad, LANE), 0) * CMP_STRIDE
    j_o = _iota((n_cmp_pad, LANE), 1) * SEL_BLOCK
    overlap = ((r_o < j_o + SEL_BLOCK) & (r_o + CMP_LEN > j_o)
               & (_iota((n_cmp_pad, LANE), 0) < n_cmp_pad - 1)).astype(BF16)
    imp_t = _dot_split_lhs(p_sum, overlap).T
    j_t = _iota((LANE, tq), 0)
    t_t = t0 + _iota((LANE, tq), 1)
    blk = t_t // SEL_BLOCK
    forced = (j_t == 0) | (j_t == blk) | (j_t == blk - 1)
    causal = (j_t * SEL_BLOCK <= t_t) & (j_t < n_sel)
    val = jnp.where(causal, jnp.where(forced, FORCE_SCORE, imp_t), -jnp.inf)
    rank = jnp.zeros((LANE, tq), jnp.int32)
    for jp in range(n_sel):
        vj = val[jp:jp + 1, :]
        beats = (vj > val) | ((vj == val) & (j_t > jp))
        rank = rank + beats.astype(jnp.int32)
    sel_t = (causal & (rank < n_top)).astype(F32)
    sel = sel_t.T.astype(BF16)
    expand = (_iota((LANE, S), 0) == _iota((LANE, S), 1) // SEL_BLOCK).astype(BF16)
    sel_keys = jnp.dot(sel, expand, preferred_element_type=F32)

    t_s = t0 + _iota((tq, S), 0)
    k_s = _iota((tq, S), 1)
    ok_sel = (sel_keys > 0.5) & (k_s <= t_s)
    o_sel = attend(ksw_ref[:, 0:SLOT], ksw_ref[:, SLOT:2 * SLOT], ok_sel, S)

    start = pl.multiple_of(jnp.clip(t0 - WINDOW, 0, S - n_win), tq)
    k_win = ksw_ref[pl.ds(start, n_win), 2 * SLOT:3 * SLOT]
    v_win = ksw_ref[pl.ds(start, n_win), 3 * SLOT:4 * SLOT]
    rel = (t0 + _iota((tq, n_win), 0)) - (start + _iota((tq, n_win), 1))
    ok_win = (rel >= 0) & (rel < WINDOW)
    o_win = attend(k_win, v_win, ok_win, n_win)

    gate = _sigmoid(g_ref[...])
    n_g = 3 * NSA_HEADS
    e_g = (_iota((LANE, n_g * SLOT), 0) == _iota((LANE, n_g * SLOT), 1) // SLOT).astype(BF16)
    g_e = _dot_split_lhs(gate, e_g)
    outs = []
    for h in range(NSA_HEADS):
        g0 = g_e[:, (3 * h) * SLOT:(3 * h + 1) * SLOT]
        g1 = g_e[:, (3 * h + 1) * SLOT:(3 * h + 2) * SLOT]
        g2 = g_e[:, (3 * h + 2) * SLOT:(3 * h + 3) * SLOT]
        rows = slice(h * tq, (h + 1) * tq)
        outs.append(g0 * o_cmp[rows, :] + g1 * o_sel[rows, :] + g2 * o_win[rows, :])
    o = jnp.concatenate(outs, axis=1)
    o_ref[...] = _rms(o, ow_ref[...], NSA_WIDTH)


def _nsa_attn(q_r, k_cmp, v_cmp, ksw, gate, out_w, B, S):
    T = B * S
    nq = S // Q_TILE
    nseg = S // CMP_STRIDE
    row = lambda b, i: (b * nq + i, 0)
    per_b = lambda b, i: (b, 0)
    return pl.pallas_call(
        functools.partial(_nsa_attn_kernel, S=S),
        grid=(B, nq),
        in_specs=[pl.BlockSpec((Q_TILE, NSA_HEADS * SLOT), row),
                  pl.BlockSpec((nseg, SLOT), per_b),
                  pl.BlockSpec((nseg, SLOT), per_b),
                  pl.BlockSpec((S, 4 * SLOT), per_b),
                  pl.BlockSpec((Q_TILE, LANE), row),
                  pl.BlockSpec((1, NSA_HEADS * SLOT), lambda b, i: (0, 0))],
        out_specs=pl.BlockSpec((Q_TILE, NSA_HEADS * SLOT), row),
        out_shape=jax.ShapeDtypeStruct((T, NSA_HEADS * SLOT), F32),
        compiler_params=_cparams("parallel", "parallel"),
    )(q_r, k_cmp, v_cmp, ksw, gate, _pad_heads_vec(out_w, NSA_HEADS))


def _outproj_kernel(x_ref, yssd_ref, ysb_ref, ynsa_ref, sbw_ref, w_ref, o_ref):
    ysb = _rms(ysb_ref[...], sbw_ref[...], SB_WIDTH)
    w0 = SSD_WIDTH
    w1 = w0 + SB_HEADS * SLOT
    w2 = w1 + NSA_HEADS * SLOT
    acc = x_ref[...]
    acc = acc + jnp.dot(yssd_ref[...].astype(BF16), w_ref[0:w0, :], preferred_element_type=F32)
    acc = acc + jnp.dot(ysb.astype(BF16), w_ref[w0:w1, :], preferred_element_type=F32)
    acc = acc + jnp.dot(ynsa_ref[...].astype(BF16), w_ref[w1:w2, :], preferred_element_type=F32)
    o_ref[...] = acc


def _outproj_rowmap():
    src = list(range(SSD_WIDTH))
    for base, nheads in ((SSD_WIDTH, SB_HEADS), (SSD_WIDTH + SB_WIDTH, NSA_HEADS)):
        for h in range(nheads):
            src.extend(range(base + h * HEAD_DIM, base + (h + 1) * HEAD_DIM))
            src.extend([-1] * (SLOT - HEAD_DIM))
    return np.asarray(src, np.int32)


_OUT_ROWMAP = _outproj_rowmap()


def _outproj(xt, y_ssd, y_sb, y_nsa, sb_out_w, w_pad):
    T = xt.shape[0]
    tm = ROW_TILE
    kdim = _OUT_ROWMAP.shape[0]
    row = lambda i: (i, 0)
    fixed = lambda i: (0, 0)
    return pl.pallas_call(
        _outproj_kernel,
        grid=(T // tm,),
        in_specs=[pl.BlockSpec((tm, D_MODEL), row),
                  pl.BlockSpec((tm, SSD_WIDTH), row),
                  pl.BlockSpec((tm, SB_HEADS * SLOT), row),
                  pl.BlockSpec((tm, NSA_HEADS * SLOT), row),
                  pl.BlockSpec((1, SB_HEADS * SLOT), fixed),
                  pl.BlockSpec((kdim, D_MODEL), fixed)],
        out_specs=pl.BlockSpec((tm, D_MODEL), row),
        out_shape=jax.ShapeDtypeStruct((T, D_MODEL), F32),
        compiler_params=_cparams("parallel"),
    )(xt, y_ssd, y_sb, y_nsa, _pad_heads_vec(sb_out_w, SB_HEADS), w_pad)


def _cmp_exchange(x, y):
    if y is None:
        return x, None
    if x is None:
        return y, None
    return jnp.maximum(x, y), jnp.minimum(x, y)


def _bitonic_sort_desc(u):
    u = list(u)
    n = len(u)
    k = 2
    while k <= n:
        j = k // 2
        while j >= 1:
            for i in range(n):
                l = i ^ j
                if l > i:
                    hi, lo = _cmp_exchange(u[i], u[l])
                    u[i], u[l] = (hi, lo) if (i & k) == 0 else (lo, hi)
            j //= 2
        k *= 2
    return u


def _merge_top(a, b):
    n = len(a)
    c = [_cmp_exchange(a[i], b[n - 1 - i])[0] for i in range(n)]
    d = n // 2
    while d >= 1:
        for i in range(n):
            if (i & d) == 0:
                c[i], c[i + d] = _cmp_exchange(c[i], c[i + d])
        d //= 2
    return c


def _top16_replicated(s):
    groups = s.shape[0] // SUBLANE
    u = _bitonic_sort_desc([s[g * SUBLANE:(g + 1) * SUBLANE, :] for g in range(groups)])
    shift = SUBLANE // 2
    while shift >= 1:
        u = _merge_top(u, [pltpu.roll(x, shift, 0) for x in u])
        shift //= 2
    return u


def _bf16_pair_words(x):
    bits = lax.bitcast_convert_type(x.astype(BF16).astype(F32), jnp.uint32)
    return bits | (bits >> 16)


def _peer_select_kernel(x_ref, nw_ref, wq_ref, sk_ref,
                        h_ref, r_ref, e1_ref, rk_ref, e2_ref):
    K = PEER_TOPK
    half = PEER_QDIM // 2
    nk = N_KEYS
    h = _rms(x_ref[...], nw_ref[...], D_MODEL)
    h_ref[...] = h.T.astype(BF16)
    q_t = _dot_nt(wq_ref[...], h)
    groups = nk // SUBLANE
    tokens = q_t.shape[1]
    for hd, t0 in [(hd, t0) for hd in range(PEER_HEADS) for t0 in range(0, tokens, LANE)]:
        lanes = slice(t0, t0 + LANE)
        r0 = hd * 2 * half
        if t0 == 0:
            s1_all = _dot(sk_ref[0], q_t[r0:r0 + half, :])
            s2_all = _dot(sk_ref[1], q_t[r0 + half:r0 + 2 * half, :])
        s1, s2 = s1_all[:, lanes], s2_all[:, lanes]
        sv1 = _top16_replicated(s1)
        sv2 = _top16_replicated(s2)
        lists = []
        for a in range(K):
            row = [sv1[a] + sv2[b] for b in range(K // (a + 1))]
            lists.append(row + [None] * (K - len(row)))
        while len(lists) > 1:
            nxt = [_merge_top(lists[i], lists[i + 1]) for i in range(0, len(lists) - 1, 2)]
            if len(lists) % 2:
                nxt.append(lists[-1])
            lists = nxt
        cv = lists[0]
        tau = cv[K - 1]
        zsum = None
        for k in range(K):
            e = jnp.exp(cv[k] - cv[0])
            zsum = e if zsum is None else zsum + e
        inv_z = 1.0 / zsum
        r_parts, rk_parts, e1_parts, e2_parts = [], [], [], []
        for g in range(groups):
            rows = slice(g * SUBLANE, (g + 1) * SUBLANE)
            a1, a2 = s1[rows, :], s2[rows, :]
            cnt = jnp.full(a1.shape, -1.0, F32)
            rk = jnp.zeros(a2.shape, F32)
            for b in range(K):
                cnt = cnt + jnp.where(a1 + sv2[b] >= tau, 1.0, 0.0)
                rk = rk + jnp.where(sv2[b] > a2, 1.0, 0.0)
            r_parts.append(cnt)
            rk_parts.append(rk)
            e1_parts.append(jnp.exp(a1 - sv1[0]))
            e2_parts.append(jnp.exp(a2 - sv2[0]) * inv_z)
        r_ref[hd, :, lanes] = _bf16_pair_words(jnp.concatenate(r_parts, axis=0))
        e1_ref[hd, :, lanes] = _bf16_pair_words(jnp.concatenate(e1_parts, axis=0))
        rk_ref[hd, :, lanes] = jnp.concatenate(rk_parts, axis=0).astype(BF16)
        e2_ref[hd, :, lanes] = jnp.concatenate(e2_parts, axis=0).astype(BF16)


BF16_ROWS = 2 * SUBLANE
GATE_LANES = 2 * LANE


def _peer_dense_kernel(h_ref, x_ref, u_ref, vt_ref, r_ref, e1_ref, rk_ref, e2_ref,
                       o_ref, acc_ref, p_ref):
    c = pl.program_id(1)
    nk = N_KEYS
    tb = h_ref.shape[1]

    @pl.when(c == 0)
    def _():
        acc_ref[...] = jnp.zeros_like(acc_ref)

    h_t = jnp.dot(u_ref[...], h_ref[...], preferred_element_type=F32)
    row0 = pl.multiple_of(c * SUBLANE, SUBLANE)
    r_rows = [r_ref[hd, pl.ds(row0, SUBLANE), :] for hd in range(PEER_HEADS)]
    e1_rows = [e1_ref[hd, pl.ds(row0, SUBLANE), :] for hd in range(PEER_HEADS)]
    tw = GATE_LANES
    for t0 in range(0, tb, tw):
        lanes = slice(t0, t0 + tw)
        for il in range(SUBLANE):
            a = h_t[il * nk:(il + 1) * nk, lanes]
            act = (0.5 * a * (1.0 + lax.erf(a * (2.0 ** -0.5)))).astype(BF16)
            rb = [pltpu.bitcast(jnp.broadcast_to(r_rows[hd][il:il + 1, lanes], (SUBLANE, tw)), BF16)
                  for hd in range(PEER_HEADS)]
            eb = [pltpu.bitcast(jnp.broadcast_to(e1_rows[hd][il:il + 1, lanes], (SUBLANE, tw)), BF16)
                  for hd in range(PEER_HEADS)]
            for j in range(nk // BF16_ROWS):
                rows = pl.ds(j * BF16_ROWS, BF16_ROWS)
                w = None
                for hd in range(PEER_HEADS):
                    t = jnp.where(rk_ref[hd, rows, lanes] <= rb[hd], e2_ref[hd, rows, lanes],
                                  jnp.zeros((), BF16)) * eb[hd]
                    w = t if w is None else w + t
                p_ref[pl.ds(il * nk + j * BF16_ROWS, BF16_ROWS), lanes] = (
                    w * act[j * BF16_ROWS:(j + 1) * BF16_ROWS, :])
    acc_ref[...] += jnp.dot(vt_ref[...], p_ref[...], preferred_element_type=F32)

    @pl.when(c == pl.num_programs(1) - 1)
    def _():
        o_ref[...] = x_ref[...] + acc_ref[...].T


def _peer(x1, norm_w, wq_t, sub_keys, u_bf, vt_bf):
    T = x1.shape[0]
    H, nk = PEER_HEADS, N_KEYS
    tb_s = SELECT_TILE
    row = lambda i: (i, 0)
    fixed2 = lambda i: (0, 0)
    tok3 = lambda i: (0, 0, i)
    head_words = jax.ShapeDtypeStruct((H, nk, T), jnp.uint32)
    head_bf16 = jax.ShapeDtypeStruct((H, nk, T), BF16)
    h_bf, r_cnt, e1, rk2, e2 = pl.pallas_call(
        _peer_select_kernel,
        grid=(T // tb_s,),
        in_specs=[pl.BlockSpec((tb_s, D_MODEL), row),
                  pl.BlockSpec((1, D_MODEL), fixed2),
                  pl.BlockSpec((H * PEER_QDIM, D_MODEL), fixed2),
                  pl.BlockSpec((2, nk, PEER_QDIM // 2), lambda i: (0, 0, 0))],
        out_specs=[pl.BlockSpec((D_MODEL, tb_s), lambda i: (0, i))] + [pl.BlockSpec((H, nk, tb_s), tok3)] * 4,
        out_shape=[jax.ShapeDtypeStruct((D_MODEL, T), BF16), head_words, head_words, head_bf16, head_bf16],
        compiler_params=_cparams("parallel"),
    )(x1, norm_w.reshape(1, D_MODEL), wq_t, sub_keys)

    tb = DENSE_TILE
    chunk = SUBLANE
    once = pl.Buffered(1)
    gate_spec = pl.BlockSpec((H, nk, tb), lambda i, c: (0, 0, i), pipeline_mode=once)
    return pl.pallas_call(
        _peer_dense_kernel,
        grid=(T // tb, nk // chunk),
        in_specs=[pl.BlockSpec((D_MODEL, tb), lambda i, c: (0, i), pipeline_mode=once),
                  pl.BlockSpec((tb, D_MODEL), lambda i, c: (i, 0), pipeline_mode=once),
                  pl.BlockSpec((chunk * nk, D_MODEL), lambda i, c: (c, 0)),
                  pl.BlockSpec((D_MODEL, chunk * nk), lambda i, c: (0, c))]
                 + [gate_spec] * 4,
        out_specs=pl.BlockSpec((tb, D_MODEL), lambda i, c: (i, 0)),
        out_shape=jax.ShapeDtypeStruct((T, D_MODEL), F32),
        scratch_shapes=[pltpu.VMEM((D_MODEL, tb), F32),
                        pltpu.VMEM((chunk * nk, tb), BF16)],
        compiler_params=_cparams("parallel", "arbitrary"),
    )(h_bf, x1, u_bf, vt_bf, r_cnt, e1, rk2, e2)


def kernel(x, norm1_w, w_in, conv_w, conv_b, dt_bias, a_log, d_skip, ssd_norm_w, sb_out_w,
           nsa_q_norm_w, nsa_k_norm_w, nsa_cmp_pos, nsa_cmp_w, nsa_out_w, w_out, norm2_w,
           peer_w_q, peer_sub_keys, peer_u, peer_v):
    B, S, _ = x.shape
    depth = w_in.shape[0]
    assert S == 2048, "NSA tiling is laid out for 2048-token sequences"
    T = B * S
    ropes = _rope_tables(S)
    xt = x.reshape(T, D_MODEL)
    for l in range(depth):
        w_in_pad = _pad_cols(w_in[l].astype(BF16), _IN_COLMAP)
        z, xbc, dt, sbq, sbk, sbv, nq, nkv, gate = _inproj(xt, norm1_w[l], w_in_pad)
        y_ssd = _ssd(z, xbc, dt, conv_w[l], conv_b[l], dt_bias[l], a_log[l], d_skip[l], ssd_norm_w[l], B, S)
        y_sb = _sb(sbq, sbk, sbv, B, S)
        q_r, kc, vc, ksw = _nsa_prep(nq, nkv, ropes, nsa_q_norm_w[l], nsa_k_norm_w[l], B, S)
        k_cmp, v_cmp = _nsa_cmp(kc, vc, nsa_cmp_pos[l], nsa_cmp_w[l], nsa_k_norm_w[l, 0], B, S)
        y_nsa = _nsa_attn(q_r, k_cmp, v_cmp, ksw, gate, nsa_out_w[l], B, S)
        w_out_pad = _pad_cols(w_out[l].astype(BF16).T, _OUT_ROWMAP).T
        x1 = _outproj(xt, y_ssd, y_sb, y_nsa, sb_out_w[l], w_out_pad)
        xt = _peer(x1, norm2_w[l], peer_w_q[l].T.astype(BF16), peer_sub_keys[l],
                   peer_u[l].astype(BF16), peer_v[l].T.astype(BF16))
    return xt.reshape(B, S, D_MODEL)
```
